```python
import math, functools
import jax, jax.numpy as jnp
from jax import lax
import numpy as np

D_MODEL = 1024
BATCH = 4
SEQ = 4096
DEPTH = 4
DEC_BATCH = 128
DEC_SEQ = 8
PAST_LEN = 2048
PAGE_SIZE = 128

DA_HEADS = 4
DA_HEAD_DIM = 64
MB_HEADS = 8
MB_HEAD_DIM = 64
MB_BLOCK = 256
MB_TOPK = 3
ROPE_THETA = 500000.0
ROPE_DIM_DA = DA_HEAD_DIM // 4
ROPE_DIM_MB = MB_HEAD_DIM // 4
PEER_HEADS = 8
PEER_NKEYS = 128
PEER_EXPERTS = PEER_NKEYS * PEER_NKEYS
PEER_QDIM = 128
PEER_TOPK = 16
LN_EPS = 1e-5
DEEPNORM_ALPHA = (2 * DEPTH) ** 0.25
DEEPNORM_BETA = (8 * DEPTH) ** -0.25
Q_BLOCK = 128
MB_Q_CHUNK = 64
PEER_TOKEN_CHUNK = 256

DA_QK_W = DA_HEADS * 2 * DA_HEAD_DIM
DA_V_W = DA_HEADS * 2 * DA_HEAD_DIM
MB_W = MB_HEADS * MB_HEAD_DIM
IN_W = 2 * DA_QK_W + DA_V_W + 3 * MB_W + 2 * D_MODEL

kernel_name = 'hybrid_diffattn_moba_peer_step'


def layer_norm(x, g, b):
    xf = x.astype(jnp.float32)
    mu = jnp.mean(xf, -1, keepdims=True)
    var = jnp.mean(jnp.square(xf - mu), -1, keepdims=True)
    return ((xf - mu) * lax.rsqrt(var + LN_EPS)).astype(x.dtype) * g + b


def rms_norm(x, g):
    xf = x.astype(jnp.float32)
    return (xf * lax.rsqrt(jnp.mean(xf * xf, -1, keepdims=True) + LN_EPS)).astype(x.dtype) * g


def partial_rope(x, pos, rot_dim):
    half = rot_dim // 2
    inv_freq = ROPE_THETA ** (-jnp.arange(half, dtype=jnp.float32) / half)
    ang = pos.astype(jnp.float32)[:, None] * inv_freq[None, :]
    shape = (1, pos.shape[0]) + (1,) * (x.ndim - 3) + (half,)
    cos = jnp.cos(ang).reshape(shape).astype(x.dtype)
    sin = jnp.sin(ang).reshape(shape).astype(x.dtype)
    x1 = x[..., :half]
    x2 = x[..., half:rot_dim]
    return jnp.concatenate([x1 * cos - x2 * sin, x1 * sin + x2 * cos, x[..., rot_dim:]], axis=-1)


def diff_attend_one(q, k, v, q_pos, lam, subln_g, lam_init):
    Tq = q.shape[0]
    L = k.shape[0]
    blk = math.gcd(Tq, Q_BLOCK)
    nb = Tq // blk
    k_pos = jnp.arange(L)
    scale = DA_HEAD_DIM ** -0.5

    def one_block(args):
        qb, pb = args
        s = jnp.einsum('qhcd,khcd->hcqk', qb, k).astype(jnp.float32) * scale
        s = jnp.where(k_pos[None, None, None, :] <= pb[None, None, :, None], s, -jnp.inf)
        p = jax.nn.softmax(s, axis=-1)
        a = (p[:, 0] - lam * p[:, 1]).astype(v.dtype)
        o = jnp.einsum('hqk,khe->qhe', a, v)
        return rms_norm(o, subln_g) * (1.0 - lam_init)

    out = lax.map(one_block, (q.reshape((nb, blk) + q.shape[1:]), q_pos.reshape(nb, blk)))
    return out.reshape(Tq, DA_HEADS, 2 * DA_HEAD_DIM)


def moba_attend_one(q, k, v, q_pos):
    Tq = q.shape[0]
    L = k.shape[0]
    nblk = -(-L // MB_BLOCK)
    pad = nblk * MB_BLOCK - L
    kb = jnp.pad(k, ((0, pad), (0, 0), (0, 0))).reshape(nblk, MB_BLOCK, MB_HEADS, MB_HEAD_DIM).transpose(2, 0, 1, 3)
    vb = jnp.pad(v, ((0, pad), (0, 0), (0, 0))).reshape(nblk, MB_BLOCK, MB_HEADS, MB_HEAD_DIM).transpose(2, 0, 1, 3)
    counts = jnp.clip(L - jnp.arange(nblk) * MB_BLOCK, 1, MB_BLOCK).astype(jnp.float32)
    kmean = (jnp.sum(kb.astype(jnp.float32), axis=2) / counts[None, :, None]).astype(k.dtype)
    kk = min(MB_TOPK, nblk)
    chunk = math.gcd(Tq, MB_Q_CHUNK)
    head_ix = jnp.arange(MB_HEADS)[:, None, None]
    offs = jnp.arange(MB_BLOCK)
    scale = MB_HEAD_DIM ** -0.5

    def one_chunk(args):
        qc, pc = args
        C = qc.shape[0]
        own = pc // MB_BLOCK
        gate = jnp.einsum('qhd,hnd->hqn', qc, kmean).astype(jnp.float32)
        fully_past = jnp.arange(nblk)[None, None, :] < own[None, :, None]
        gate = jnp.where(fully_past, gate, -jnp.inf)
        _, top = lax.top_k(gate, kk)
        valid = jnp.arange(kk)[None, :] < own[:, None]
        sel = jnp.concatenate([top, jnp.broadcast_to(own[None, :, None], (MB_HEADS, C, 1))], axis=-1)
        ok = jnp.concatenate([jnp.broadcast_to(valid[None], (MB_HEADS, C, kk)),
                              jnp.ones((MB_HEADS, C, 1), dtype=bool)], axis=-1)
        kg = kb[head_ix, sel]
        vg = vb[head_ix, sel]
        kpos = sel[..., None] * MB_BLOCK + offs
        mask = ok[..., None] & (kpos <= pc[None, :, None, None])
        s = jnp.einsum('qhd,hqsjd->hqsj', qc, kg).astype(jnp.float32) * scale
        s = jnp.where(mask, s, -jnp.inf).reshape(MB_HEADS, C, -1)
        p = jax.nn.softmax(s, axis=-1).reshape(mask.shape).astype(v.dtype)
        return jnp.einsum('hqsj,hqsjd->qhd', p, vg)

    out = lax.map(one_chunk, (q.reshape(Tq // chunk, chunk, MB_HEADS, MB_HEAD_DIM), q_pos.reshape(Tq // chunk, chunk)))
    return out.reshape(Tq, MB_HEADS, MB_HEAD_DIM)


def peer_ffn(h, wq, keys, u_tab, v_tab):
    N = h.shape[0]
    chunk = math.gcd(N, PEER_TOKEN_CHUNK)
    half = PEER_QDIM // 2
    K = PEER_TOPK

    def one_chunk(hc):
        C = hc.shape[0]
        q = (hc @ wq).reshape(C, PEER_HEADS, 2, half)
        s = jnp.einsum('thcd,hcnd->thcn', q, keys).astype(jnp.float32)
        sc, ix = lax.top_k(s, K)
        cand = (sc[:, :, 0, :, None] + sc[:, :, 1, None, :]).reshape(C, PEER_HEADS, K * K)
        cand_ix = (ix[:, :, 0, :, None] * PEER_NKEYS + ix[:, :, 1, None, :]).reshape(C, PEER_HEADS, K * K)
        top_s, pos = lax.top_k(cand, K)
        expert = jnp.take_along_axis(cand_ix, pos, axis=-1)
        g = jax.nn.softmax(top_s, axis=-1).astype(hc.dtype)
        a = jax.nn.gelu(jnp.einsum('td,thkd->thk', hc, u_tab[expert]), approximate=False)
        return jnp.einsum('thk,thkd->td', g * a, v_tab[expert])

    return lax.map(one_chunk, h.reshape(N // chunk, chunk, D_MODEL)).reshape(N, D_MODEL)


def mixer_sublayer(h, pos, w_in, w_branch_a, w_branch_b, w_out, attend):
    Bg, T, _ = h.shape
    z = h @ w_in
    pts = []
    acc = 0
    for w in (DA_QK_W, DA_QK_W, DA_V_W, MB_W, MB_W, MB_W, D_MODEL):
        acc += w
        pts.append(acc)
    qa, ka, va, qb, kb, vb, ga, gb = jnp.split(z, pts, axis=-1)
    qa = partial_rope(qa.reshape(Bg, T, DA_HEADS, 2, DA_HEAD_DIM), pos, ROPE_DIM_DA)
    ka = partial_rope(ka.reshape(Bg, T, DA_HEADS, 2, DA_HEAD_DIM), pos, ROPE_DIM_DA)
    va = va.reshape(Bg, T, DA_HEADS, 2 * DA_HEAD_DIM)
    qb = partial_rope(qb.reshape(Bg, T, MB_HEADS, MB_HEAD_DIM), pos, ROPE_DIM_MB)
    kb = partial_rope(kb.reshape(Bg, T, MB_HEADS, MB_HEAD_DIM), pos, ROPE_DIM_MB)
    vb = vb.reshape(Bg, T, MB_HEADS, MB_HEAD_DIM)
    oa, ob = attend(qa, ka, va, qb, kb, vb)
    merged = (jax.nn.sigmoid(ga) * (oa.reshape(Bg, T, -1) @ w_branch_a)
              + jax.nn.sigmoid(gb) * (ob.reshape(Bg, T, -1) @ w_branch_b))
    new_rows = (ka.reshape(Bg, T, DA_HEADS, 2 * DA_HEAD_DIM), va, kb, vb)
    return merged @ w_out, new_rows


def setup_inputs(seed: int = 0) -> dict:
    key = jax.random.key(seed)
    ks = jax.random.split(key, 40)
    f32 = jnp.float32

    def nrm(k, shape, s=1.0):
        return jax.random.normal(k, shape, f32) * s

    n_pages = PAST_LEN // PAGE_SIZE
    n_used = DEC_BATCH * n_pages
    n_pool = n_used + max(1, n_used // 4)
    da_e = 2 * DA_HEAD_DIM
    page_table = jax.random.permutation(ks[8], n_pool)[:n_used].reshape(DEC_BATCH, n_pages).astype(jnp.int32)
    d_inv = D_MODEL ** -0.5
    return {
        'x_prompt': nrm(ks[0], (BATCH, SEQ, D_MODEL)),
        'x_sample': nrm(ks[1], (DEC_BATCH, DEC_SEQ, D_MODEL)),
        'cache_dk': nrm(ks[2], (n_pool, DEPTH, PAGE_SIZE, DA_HEADS, da_e)),
        'cache_dv': nrm(ks[3], (n_pool, DEPTH, PAGE_SIZE, DA_HEADS, da_e)),
        'cache_mk': nrm(ks[4], (n_pool, DEPTH, PAGE_SIZE, MB_HEADS, MB_HEAD_DIM)),
        'cache_mv': nrm(ks[5], (n_pool, DEPTH, PAGE_SIZE, MB_HEADS, MB_HEAD_DIM)),
        'page_table': page_table,
        'c_prompt': nrm(ks[6], (BATCH, D_MODEL)),
        'c_sample': nrm(ks[7], (DEC_BATCH, D_MODEL)),
        'ln_in_g': 1.0 + nrm(ks[9], (D_MODEL,), 0.02),
        'ln_in_b': nrm(ks[10], (D_MODEL,), 0.02),
        'w_ada_mix': nrm(ks[11], (DEPTH, D_MODEL, 3 * D_MODEL), d_inv),
        'b_ada_mix': nrm(ks[12], (DEPTH, 3 * D_MODEL), 0.01),
        'w_in': nrm(ks[13], (DEPTH, D_MODEL, IN_W), d_inv),
        'diff_lambda': nrm(ks[14], (DEPTH, 4, DA_HEAD_DIM), 0.1),
        'diff_subln_g': 1.0 + nrm(ks[15], (DEPTH, da_e), 0.02),
        'w_branch_a': nrm(ks[16], (DEPTH, DA_V_W, D_MODEL), DA_V_W ** -0.5),
        'w_branch_b': nrm(ks[17], (DEPTH, MB_W, D_MODEL), MB_W ** -0.5),
        'w_out': nrm(ks[18], (DEPTH, D_MODEL, D_MODEL), d_inv * DEEPNORM_BETA),
        'ln1_g': 1.0 + nrm(ks[19], (DEPTH, D_MODEL), 0.02),
        'ln1_b': nrm(ks[20], (DEPTH, D_MODEL), 0.02),
        'w_ada_ffn': nrm(ks[21], (DEPTH, D_MODEL, 3 * D_MODEL), d_inv),
        'b_ada_ffn': nrm(ks[22], (DEPTH, 3 * D_MODEL), 0.01),
        'peer_wq': nrm(ks[23], (DEPTH, D_MODEL, PEER_HEADS * PEER_QDIM), d_inv),
        'peer_keys': nrm(ks[24], (DEPTH, PEER_HEADS, 2, PEER_NKEYS, PEER_QDIM // 2), (PEER_QDIM // 2) ** -0.5),
        'peer_u': nrm(ks[25], (DEPTH, PEER_EXPERTS, D_MODEL), d_inv),
        'peer_v': nrm(ks[26], (DEPTH, PEER_EXPERTS, D_MODEL), DEEPNORM_BETA),
        'ln2_g': 1.0 + nrm(ks[27], (DEPTH, D_MODEL), 0.02),
        'ln2_b': nrm(ks[28], (DEPTH, D_MODEL), 0.02),
    }


def reference(x_prompt, x_sample, cache_dk, cache_dv, cache_mk, cache_mv, page_table, c_prompt, c_sample,
              ln_in_g, ln_in_b, w_ada_mix, b_ada_mix, w_in, diff_lambda, diff_subln_g, w_branch_a, w_branch_b,
              w_out, ln1_g, ln1_b, w_ada_ffn, b_ada_ffn, peer_wq, peer_keys, peer_u, peer_v, ln2_g, ln2_b):
    pos_prompt = jnp.arange(SEQ)
    pos_sample = PAST_LEN + jnp.arange(DEC_SEQ)

    def attend_prompt(l, lam, lam_init, subln_g, qa, ka, va, qb, kb, vb):
        oa = lax.map(lambda a: diff_attend_one(a[0], a[1], a[2], pos_prompt, lam, subln_g, lam_init), (qa, ka, va))
        ob = lax.map(lambda a: moba_attend_one(a[0], a[1], a[2], pos_prompt), (qb, kb, vb))
        return oa, ob

    def attend_sample(l, lam, lam_init, subln_g, qa, ka, va, qb, kb, vb):
        def da_one(a):
            q, k, v, pt = a
            k_past = cache_dk[pt, l].reshape(-1, DA_HEADS, 2, DA_HEAD_DIM)
            v_past = cache_dv[pt, l].reshape(-1, DA_HEADS, 2 * DA_HEAD_DIM)
            return diff_attend_one(q, jnp.concatenate([k_past, k], axis=0), jnp.concatenate([v_past, v], axis=0),
                                   pos_sample, lam, subln_g, lam_init)

        def mb_one(a):
            q, k, v, pt = a
            k_past = cache_mk[pt, l].reshape(-1, MB_HEADS, MB_HEAD_DIM)
            v_past = cache_mv[pt, l].reshape(-1, MB_HEADS, MB_HEAD_DIM)
            return moba_attend_one(q, jnp.concatenate([k_past, k], axis=0), jnp.concatenate([v_past, v], axis=0),
                                   pos_sample)

        oa = lax.map(da_one, (qa, ka, va, page_table))
        ob = lax.map(mb_one, (qb, kb, vb, page_table))
        return oa, ob

    def run_group(x, c, pos, attend):
        x = layer_norm(x, ln_in_g, ln_in_b)
        cs = jax.nn.silu(c)
        dk, dv, mk, mv = [], [], [], []
        for l in range(DEPTH):
            lam_init = 0.8 - 0.6 * math.exp(-0.3 * l)
            lp = diff_lambda[l].astype(jnp.float32)
            lam = jnp.exp(jnp.sum(lp[0] * lp[1])) - jnp.exp(jnp.sum(lp[2] * lp[3])) + lam_init
            sh1, sc1, gt1 = jnp.split(cs @ w_ada_mix[l] + b_ada_mix[l], 3, axis=-1)
            h = x * (1 + sc1[:, None]) + sh1[:, None]
            y, rows = mixer_sublayer(h, pos, w_in[l], w_branch_a[l], w_branch_b[l], w_out[l],
                                     functools.partial(attend, l, lam, lam_init, diff_subln_g[l]))
            x = layer_norm(DEEPNORM_ALPHA * x + gt1[:, None] * y, ln1_g[l], ln1_b[l])
            sh2, sc2, gt2 = jnp.split(cs @ w_ada_ffn[l] + b_ada_ffn[l], 3, axis=-1)
            h2 = x * (1 + sc2[:, None]) + sh2[:, None]
            f = peer_ffn(h2.reshape(-1, D_MODEL), peer_wq[l], peer_keys[l], peer_u[l], peer_v[l]).reshape(x.shape)
            x = layer_norm(DEEPNORM_ALPHA * x + gt2[:, None] * f, ln2_g[l], ln2_b[l])
            dk.append(rows[0])
            dv.append(rows[1])
            mk.append(rows[2])
            mv.append(rows[3])
        return (x, jnp.stack(dk, axis=1), jnp.stack(dv, axis=1), jnp.stack(mk, axis=1), jnp.stack(mv, axis=1))

    y_prompt, dk_p, dv_p, mk_p, mv_p = run_group(x_prompt, c_prompt, pos_prompt, attend_prompt)
    y_sample, dk_s, dv_s, mk_s, mv_s = run_group(x_sample, c_sample, pos_sample, attend_sample)
    return (y_prompt, y_sample, dk_p, dv_p, mk_p, mv_p, dk_s, dv_s, mk_s, mv_s)
```

```python
import functools
import math

import numpy as np
import jax
import jax.numpy as jnp
from jax import lax
from jax.experimental import pallas as pl
from jax.experimental.pallas import tpu as pltpu

D_MODEL = 1024
DA_HEADS = 4
DA_HEAD_DIM = 64
MB_HEADS = 8
MB_HEAD_DIM = 64
MB_BLOCK = 256
MB_TOPK = 3
ROPE_THETA = 500000.0
ROPE_DIM = 16
PEER_HEADS = 8
PEER_NKEYS = 128
PEER_QDIM = 128
PEER_TOPK = 16
LN_EPS = 1e-5
QKV_W = 512
HEAD_CHUNK = 64

LANES = 128
SUBLANES = 8
VMEM_LIMIT_BYTES = 52 * 1024 * 1024

F32 = jnp.float32
BF16 = jnp.bfloat16
NEG_BIG = -1e30


def _cparams(sem):
    return pltpu.CompilerParams(dimension_semantics=sem, vmem_limit_bytes=VMEM_LIMIT_BYTES)


def _dot(a, b):
    return jnp.dot(a, b, preferred_element_type=F32)


def _dot_nt(a, b):
    return lax.dot_general(a, b, (((1,), (1,)), ((), ())), preferred_element_type=F32)


def _split(x):
    hi = x.astype(BF16)
    lo = (x - hi.astype(F32)).astype(BF16)
    return hi, lo


def _dot3(ah, al, bh, bl):
    return _dot(ah, bh) + (_dot(ah, bl) + _dot(al, bh))


def _dot3_nt(ah, al, bh, bl):
    return _dot_nt(ah, bh) + (_dot_nt(ah, bl) + _dot_nt(al, bh))


def _layer_norm(x, g, b):
    mu = jnp.mean(x, axis=-1, keepdims=True)
    xc = x - mu
    var = jnp.mean(xc * xc, axis=-1, keepdims=True)
    return xc * lax.rsqrt(var + LN_EPS) * g + b


def _ada_kernel(c_ref, w_ref, b_ref, o_ref):
    c = c_ref[...]
    cs = c * (1.0 / (1.0 + jnp.exp(-c)))
    ch, cl = _split(cs)
    wh, wl = _split(w_ref[...])
    o_ref[...] = _dot3(ch, cl, wh, wl) + b_ref[...]


def _ada(c_all, w, b):
    depth, d, d3 = w.shape
    r = c_all.shape[0]
    tn = 1024
    return pl.pallas_call(
        _ada_kernel,
        grid=(depth, d3 // tn),
        in_specs=[
            pl.BlockSpec((r, d), lambda l, j: (0, 0)),
            pl.BlockSpec((None, d, tn), lambda l, j: (l, 0, j)),
            pl.BlockSpec((None, 1, tn), lambda l, j: (l, 0, j)),
        ],
        out_specs=pl.BlockSpec((None, r, tn), lambda l, j: (l, 0, j)),
        out_shape=jax.ShapeDtypeStruct((depth, r, d3), F32),
        compiler_params=_cparams(("arbitrary", "arbitrary")),
        name="ada",
    )(c_all, w, b.reshape(depth, 1, d3))


def _ln_kernel(x_ref, g_ref, b_ref, o_ref):
    o_ref[...] = _layer_norm(x_ref[...], g_ref[...], b_ref[...])


def _ln_rows(x, g, b, tm):
    n, d = x.shape
    return pl.pallas_call(
        _ln_kernel,
        grid=(n // tm,),
        in_specs=[
            pl.BlockSpec((tm, d), lambda i: (i, 0)),
            pl.BlockSpec((1, d), lambda i: (0, 0)),
            pl.BlockSpec((1, d), lambda i: (0, 0)),
        ],
        out_specs=pl.BlockSpec((tm, d), lambda i: (i, 0)),
        out_shape=jax.ShapeDtypeStruct((n, d), F32),
        compiler_params=_cparams(("arbitrary",)),
        name="ln_in",
    )(x, g.reshape(1, d), b.reshape(1, d))


def _mod_operand(mods, per_token, tokens_per_seq):
    if per_token:
        return jnp.repeat(mods, tokens_per_seq, axis=0)
    return mods.reshape(mods.shape[0], 1, mods.shape[1])


def _mod_spec(per_token, tm, tiles_per_seq, part):
    if per_token:
        return pl.BlockSpec((tm, D_MODEL), lambda i: (i, part))
    return pl.BlockSpec((None, 1, D_MODEL), lambda i: (i // tiles_per_seq, 0, part))


def _rope(z, c, s1, s2):
    reps = z.shape[1] // LANES
    c = jnp.tile(c, (1, reps))
    s1 = jnp.tile(s1, (1, reps))
    s2 = jnp.tile(s2, (1, reps))
    half = ROPE_DIM // 2
    return z * c + pltpu.roll(z, half, 1) * s1 + pltpu.roll(z, z.shape[1] - half, 1) * s2


def _inproj_kernel(x_ref, sh_ref, sc_ref, w_ref, c_ref, s1_ref, s2_ref,
                   qa_ref, ka_ref, va_ref, qb_ref, kb_ref, vb_ref,
                   qab_ref, kab_ref, vab_ref, qbb_ref, kbb_ref, vbb_ref, sg_ref):
    h = x_ref[...] * (1.0 + sc_ref[...]) + sh_ref[...]
    hb = h.astype(BF16)
    c = c_ref[...]
    s1 = s1_ref[...]
    s2 = s2_ref[...]
    f32_outs = (qa_ref, ka_ref, va_ref, qb_ref, kb_ref, vb_ref)
    bf_outs = (qab_ref, kab_ref, vab_ref, qbb_ref, kbb_ref, vbb_ref)
    rotate = (True, True, False, True, True, False)
    for j in range(6):
        z = _dot(hb, w_ref[:, j * QKV_W:(j + 1) * QKV_W])
        if rotate[j]:
            z = _rope(z, c, s1, s2)
        f32_outs[j][...] = z
        bf_outs[j][...] = z.astype(BF16)
    for j in range(4):
        z = _dot(hb, w_ref[:, (6 + j) * QKV_W:(7 + j) * QKV_W])
        sg_ref[:, j * QKV_W:(j + 1) * QKV_W] = 1.0 / (1.0 + jnp.exp(-z))


def _inproj(x, mod, per_token, tokens_per_seq, w_bf, rope_tabs, tm):
    n, d = x.shape
    in_w = w_bf.shape[1]
    tiles_per_seq = max(tokens_per_seq // tm, 1)
    ctab, s1tab, s2tab = rope_tabs
    n_tab = ctab.shape[0] // tm
    slab = pl.BlockSpec((tm, QKV_W), lambda i: (i, 0))
    tab = pl.BlockSpec((tm, LANES), lambda i: (i % n_tab, 0))
    outs = pl.pallas_call(
        _inproj_kernel,
        grid=(n // tm,),
        in_specs=[
            pl.BlockSpec((tm, d), lambda i: (i, 0)),
            _mod_spec(per_token, tm, tiles_per_seq, 0),
            _mod_spec(per_token, tm, tiles_per_seq, 1),
            pl.BlockSpec((d, in_w), lambda i: (0, 0)),
            tab, tab, tab,
        ],
        out_specs=[slab] * 12 + [pl.BlockSpec((tm, 2 * D_MODEL), lambda i: (i, 0))],
        out_shape=[jax.ShapeDtypeStruct((n, QKV_W), F32)] * 6
        + [jax.ShapeDtypeStruct((n, QKV_W), BF16)] * 6
        + [jax.ShapeDtypeStruct((n, 2 * D_MODEL), F32)],
        compiler_params=_cparams(("arbitrary",)),
        name="inproj",
    )(x, mod, mod, w_bf, ctab, s1tab, s2tab)
    return outs


def _rope_tables(pos):
    half = ROPE_DIM // 2
    inv_freq = ROPE_THETA ** (-jnp.arange(half, dtype=F32) / half)
    ang = pos.astype(F32)[:, None] * inv_freq[None, :]
    cos = jnp.cos(ang)
    sin = jnp.sin(ang)
    t = pos.shape[0]
    ones = jnp.ones((t, HEAD_CHUNK - ROPE_DIM), F32)
    zeros = jnp.zeros((t, HEAD_CHUNK - ROPE_DIM), F32)
    zh = jnp.zeros((t, half), F32)
    c = jnp.concatenate([cos, cos, ones], axis=1)
    s1 = jnp.concatenate([zh, sin, zeros], axis=1)
    s2 = jnp.concatenate([-sin, zh, zeros], axis=1)
    reps = LANES // HEAD_CHUNK
    return tuple(jnp.tile(a, (1, reps)) for a in (c, s1, s2))


def _kmean_kernel(k_ref, o_ref):
    k = k_ref[...]
    nb = k.shape[0] // MB_BLOCK
    o_ref[...] = jnp.sum(k.reshape(nb, MB_BLOCK, k.shape[1]), axis=1) * (1.0 / MB_BLOCK)


def _kmean(kb):
    n, w = kb.shape
    rows = SUBLANES * MB_BLOCK
    return pl.pallas_call(
        _kmean_kernel,
        grid=(n // rows,),
        in_specs=[pl.BlockSpec((rows, w), lambda i: (i, 0))],
        out_specs=pl.BlockSpec((SUBLANES, w), lambda i: (i, 0)),
        out_shape=jax.ShapeDtypeStruct((n // MB_BLOCK, w), F32),
        compiler_params=_cparams(("arbitrary",)),
        name="moba_kmean",
    )(kb)


def _moba_select(gate, own, nblk):
    nidx = lax.broadcasted_iota(jnp.int32, (1, nblk), 1)
    rank = jnp.zeros(gate.shape, jnp.int32)
    for n2 in range(nblk):
        col = gate[:, n2:n2 + 1]
        beats = (col > gate) | ((col == gate) & (n2 < nidx))
        rank = rank + jnp.where(beats & (n2 < own), 1, 0)
    selected = (nidx == own) | ((nidx < own) & (rank < MB_TOPK))
    return jnp.where(selected, 0.0, NEG_BIG)


def _attn_kernel(qt_ref, kt_ref, *refs, mode, tq, tk, nblk, lam_init):
    if mode == "diff":
        q_ref, k_ref, v_ref, lam_ref, g_ref, o_ref, m_sc, l_sc, acc_sc = refs
    else:
        q_ref, k_ref, v_ref, qf_ref, km_ref, o_ref, m_sc, l_sc, acc_sc, sel_sc = refs
    step = pl.program_id(2)
    qi = qt_ref[step]
    ki = kt_ref[step]
    lane = lax.broadcasted_iota(jnp.int32, (1, LANES), 1)
    lo = lane < HEAD_CHUNK
    rowpos = qi * tq + lax.broadcasted_iota(jnp.int32, (tq, 1), 0)
    colpos = ki * tk + lax.broadcasted_iota(jnp.int32, (1, tk), 1)

    @pl.when(ki == 0)
    def _init():
        m_sc[...] = jnp.full(m_sc.shape, NEG_BIG, F32)
        l_sc[...] = jnp.zeros(l_sc.shape, F32)
        acc_sc[...] = jnp.zeros(acc_sc.shape, F32)
        if mode == "moba":
            qf = qf_ref[...]
            kmh, kml = _split(km_ref[...])
            own = rowpos // MB_BLOCK
            for c in range(2):
                qc = jnp.where(lo if c == 0 else ~lo, qf, 0.0)
                qh, ql = _split(qc)
                gate = _dot3_nt(qh, ql, kmh, kml)
                sel_sc[c] = _moba_select(gate, own, nblk).astype(BF16)

    scale = DA_HEAD_DIM ** -0.5
    q = q_ref[...]
    k = k_ref[...]
    v = v_ref[...]
    causal = colpos <= rowpos
    if mode == "moba":
        blk_of_col = colpos // MB_BLOCK
        onehot = jnp.where(lax.broadcasted_iota(jnp.int32, (nblk, 1), 0) == blk_of_col, 1.0, 0.0).astype(BF16)
    for c in range(2):
        qc = jnp.where(lo if c == 0 else ~lo, q, jnp.zeros_like(q)) * scale
        s = _dot_nt(qc.astype(BF16), k)
        if mode == "moba":
            s = s + _dot(sel_sc[c], onehot)
        s = jnp.where(causal, s, NEG_BIG)
        m_prev = m_sc[c]
        m_new = jnp.maximum(m_prev, jnp.max(s, axis=1, keepdims=True))
        alpha = jnp.exp(m_prev - m_new)
        p = jnp.exp(s - m_new)
        l_sc[c] = alpha * l_sc[c] + jnp.sum(p, axis=1, keepdims=True)
        acc_sc[c] = alpha * acc_sc[c] + _dot(p.astype(BF16), v)
        m_sc[c] = m_new

    last_ki = ((qi + 1) * tq - 1) // tk

    @pl.when(ki == last_ki)
    def _finish():
        o0 = acc_sc[0] / l_sc[0]
        o1 = acc_sc[1] / l_sc[1]
        if mode == "diff":
            lp = lam_ref[...]
            lam = (jnp.exp(jnp.sum(lp[0:1] * lp[1:2], axis=1, keepdims=True))
                   - jnp.exp(jnp.sum(lp[2:3] * lp[3:4], axis=1, keepdims=True)) + lam_init)
            o = o0 - lam * o1
            o = o * lax.rsqrt(jnp.mean(o * o, axis=1, keepdims=True) + LN_EPS) * g_ref[...] * (1.0 - lam_init)
        else:
            o = jnp.where(lo, o0, o1)
        o_ref[...] = o.astype(o_ref.dtype)


def _tri_steps(t, tq, tk):
    qs, ks = [], []
    for qi in range(t // tq):
        for ki in range(((qi + 1) * tq - 1) // tk + 1):
            qs.append(qi)
            ks.append(ki)
    return np.asarray(qs, np.int32), np.asarray(ks, np.int32)


def _attn_prompt(mode, q, k, v, extra, batch, t, tq, tk, lam_init=0.0):
    n, w = q.shape
    nslab = w // LANES
    nblk = t // MB_BLOCK
    qs, ks = _tri_steps(t, tq, tk)
    nq, nk = t // tq, t // tk
    qspec = pl.BlockSpec((tq, LANES), lambda b, h, s, qt, kt: (b * nq + qt[s], h))
    kspec = pl.BlockSpec((tk, LANES), lambda b, h, s, qt, kt: (b * nk + kt[s], h))
    if mode == "diff":
        extra_specs = [pl.BlockSpec((4, DA_HEAD_DIM), lambda b, h, s, qt, kt: (0, 0)),
                       pl.BlockSpec((1, LANES), lambda b, h, s, qt, kt: (0, 0))]
        scratch = []
    else:
        extra_specs = [qspec, pl.BlockSpec((nblk, LANES), lambda b, h, s, qt, kt: (b, h))]
        scratch = [pltpu.VMEM((2, tq, nblk), BF16)]
    kern = functools.partial(_attn_kernel, mode=mode, tq=tq, tk=tk, nblk=nblk, lam_init=lam_init)
    return pl.pallas_call(
        kern,
        grid_spec=pltpu.PrefetchScalarGridSpec(
            num_scalar_prefetch=2,
            grid=(batch, nslab, len(qs)),
            in_specs=[qspec, kspec, kspec] + extra_specs,
            out_specs=qspec,
            scratch_shapes=[pltpu.VMEM((2, tq, 1), F32), pltpu.VMEM((2, tq, 1), F32),
                            pltpu.VMEM((2, tq, LANES), F32)] + scratch,
        ),
        out_shape=jax.ShapeDtypeStruct((n, w), BF16),
        compiler_params=_cparams(("arbitrary", "arbitrary", "arbitrary")),
        name="attn_" + mode,
    )(jnp.asarray(qs), jnp.asarray(ks), q, k, v, *extra)


def _stack_heads(q8, n_groups):
    t, w = q8.shape
    rows = n_groups * t
    qt = jnp.concatenate([q8] * n_groups, axis=0)
    rowgrp = lax.broadcasted_iota(jnp.int32, (rows, 1), 0) // t
    colgrp = lax.broadcasted_iota(jnp.int32, (1, w), 1) // (w // n_groups)
    diag = rowgrp == colgrp
    return jnp.where(diag, qt, 0.0), diag


def _diff_dec_kernel(pt_ref, q_ref, kn_ref, vn_ref, lam_ref, g_ref, *refs, pp, tdec, lam_init):
    k_refs = refs[:pp]
    v_refs = refs[pp:2 * pp]
    o_ref, qbd_sc, m_sc, l_sc, acc_sc = refs[2 * pp:]
    p_idx = pl.program_id(1)
    scale = DA_HEAD_DIM ** -0.5
    ngrp = 2 * DA_HEADS
    rows = ngrp * tdec

    @pl.when(p_idx == 0)
    def _init():
        qbd, _ = _stack_heads(q_ref[...] * scale, ngrp)
        qbd_sc[...] = qbd
        m_sc[...] = jnp.full(m_sc.shape, NEG_BIG, F32)
        l_sc[...] = jnp.zeros(l_sc.shape, F32)
        acc_sc[...] = jnp.zeros(acc_sc.shape, F32)

    qbd = qbd_sc[...]
    qb = qbd.astype(BF16)
    for r in range(pp):
        kp = k_refs[r][...].astype(BF16)
        vp = v_refs[r][...].astype(BF16)
        s = _dot_nt(qb, kp)
        m_prev = m_sc[...]
        m_new = jnp.maximum(m_prev, jnp.max(s, axis=1, keepdims=True))
        alpha = jnp.exp(m_prev - m_new)
        p = jnp.exp(s - m_new)
        l_sc[...] = alpha * l_sc[...] + jnp.sum(p, axis=1, keepdims=True)
        acc_sc[...] = alpha * acc_sc[...] + _dot(p.astype(BF16), vp)
        m_sc[...] = m_new

    @pl.when(p_idx == pl.num_programs(1) - 1)
    def _finish():
        s = _dot_nt(qbd, kn_ref[...])
        trow = lax.broadcasted_iota(jnp.int32, (rows, 1), 0) % tdec
        jcol = lax.broadcasted_iota(jnp.int32, (1, tdec), 1)
        s = jnp.where(jcol <= trow, s, NEG_BIG)
        m_prev = m_sc[...]
        m_new = jnp.maximum(m_prev, jnp.max(s, axis=1, keepdims=True))
        alpha = jnp.exp(m_prev - m_new)
        p = jnp.exp(s - m_new)
        l = alpha * l_sc[...] + jnp.sum(p, axis=1, keepdims=True)
        acc = alpha * acc_sc[...] + _dot(p, vn_ref[...])
        o = acc / l
        lp = lam_ref[...]
        lam = (jnp.exp(jnp.sum(lp[0:1] * lp[1:2], axis=1, keepdims=True))
               - jnp.exp(jnp.sum(lp[2:3] * lp[3:4], axis=1, keepdims=True)) + lam_init)
        rowmap = lax.broadcasted_iota(jnp.int32, (rows, 1), 0) // tdec
        colhead = lax.broadcasted_iota(jnp.int32, (1, o.shape[1]), 1) // (2 * DA_HEAD_DIM)
        coef = jnp.where(rowmap % 2 == 0, 1.0, -lam)
        o = jnp.where(rowmap // 2 == colhead, o * coef, 0.0)
        o = jnp.sum(o.reshape(ngrp, tdec, o.shape[1]), axis=0)
        outs = []
        for h in range(DA_HEADS):
            oh = o[:, h * LANES:(h + 1) * LANES]
            outs.append(oh * lax.rsqrt(jnp.mean(oh * oh, axis=1, keepdims=True) + LN_EPS))
        o = jnp.concatenate(outs, axis=1) * g_ref[...] * (1.0 - lam_init)
        o_ref[...] = o


def _page_spec(layer, pp, r, rows, w):
    return pl.BlockSpec((None, None, rows, w), lambda b, p, pt: (pt[b, p * pp + r], layer, 0, 0))


def _diff_dec(q, kn, vn, lam_p, g4, cache_k, cache_v, page_table, layer, lam_init, pp):
    ns, w = q.shape
    bs, n_pages = page_table.shape
    tdec = ns // bs
    page = cache_k.shape[2]
    rows = 2 * DA_HEADS * tdec
    row_spec = pl.BlockSpec((tdec, w), lambda b, p, pt: (b, 0))
    kern = functools.partial(_diff_dec_kernel, pp=pp, tdec=tdec, lam_init=lam_init)
    return pl.pallas_call(
        kern,
        grid_spec=pltpu.PrefetchScalarGridSpec(
            num_scalar_prefetch=1,
            grid=(bs, n_pages // pp),
            in_specs=[row_spec, row_spec, row_spec,
                      pl.BlockSpec((4, DA_HEAD_DIM), lambda b, p, pt: (0, 0)),
                      pl.BlockSpec((1, w), lambda b, p, pt: (0, 0))]
            + [_page_spec(layer, pp, r, page, w) for r in range(pp)] * 2,
            out_specs=row_spec,
            scratch_shapes=[pltpu.VMEM((rows, w), F32), pltpu.VMEM((rows, 1), F32),
                            pltpu.VMEM((rows, 1), F32), pltpu.VMEM((rows, w), F32)],
        ),
        out_shape=jax.ShapeDtypeStruct((ns, w), F32),
        compiler_params=_cparams(("arbitrary", "arbitrary")),
        name="diff_dec",
    )(page_table, q, kn, vn, lam_p, g4, *([cache_k] * pp), *([cache_v] * pp))


def _moba_dec_kernel(pt_ref, q_ref, kn_ref, vn_ref, *refs, ppb, tdec, nb):
    k_refs = refs[:ppb]
    v_refs = refs[ppb:2 * ppb]
    o_ref, qbd_sc, gate_sc, m_sc, l_sc, acc_sc = refs[2 * ppb:]
    n_idx = pl.program_id(1)
    scale = MB_HEAD_DIM ** -0.5
    rows = MB_HEADS * tdec

    @pl.when(n_idx == 0)
    def _init():
        qbd, _ = _stack_heads(q_ref[...], MB_HEADS)
        qbd_sc[...] = qbd

    qbd = qbd_sc[...]
    qb = (qbd * scale).astype(BF16)
    kblk = jnp.concatenate([k_refs[r][...] for r in range(ppb)], axis=0)
    vblk = jnp.concatenate([v_refs[r][...] for r in range(ppb)], axis=0)
    kmean = jnp.sum(kblk, axis=0, keepdims=True) * (1.0 / MB_BLOCK)
    gate = jnp.sum(qbd * kmean, axis=1, keepdims=True)
    s = _dot_nt(qb, kblk.astype(BF16))
    m_n = jnp.max(s, axis=1, keepdims=True)
    p = jnp.exp(s - m_n)
    gate_sc[n_idx] = jnp.broadcast_to(gate, (rows, LANES))
    m_sc[n_idx] = jnp.broadcast_to(m_n, (rows, LANES))
    l_sc[n_idx] = jnp.broadcast_to(jnp.sum(p, axis=1, keepdims=True), (rows, LANES))
    acc_sc[n_idx] = _dot(p.astype(BF16), vblk.astype(BF16))

    @pl.when(n_idx == nb - 1)
    def _finish():
        s_o = _dot_nt(qbd * scale, kn_ref[...])
        trow = lax.broadcasted_iota(jnp.int32, (rows, 1), 0) % tdec
        jcol = lax.broadcasted_iota(jnp.int32, (1, tdec), 1)
        s_o = jnp.where(jcol <= trow, s_o, NEG_BIG)
        m_o = jnp.max(s_o, axis=1, keepdims=True)
        p_o = jnp.exp(s_o - m_o)
        l_o = jnp.sum(p_o, axis=1, keepdims=True)
        a_o = _dot(p_o, vn_ref[...])
        gates = [gate_sc[n][:, 0:1] for n in range(nb)]
        sels = []
        for n in range(nb):
            rank = jnp.zeros((rows, 1), jnp.int32)
            for n2 in range(nb):
                if n2 == n:
                    continue
                beats = (gates[n2] > gates[n]) | ((gates[n2] == gates[n]) & (n2 < n))
                rank = rank + jnp.where(beats, 1, 0)
            sels.append(rank < MB_TOPK)
        m = m_o
        for n in range(nb):
            m = jnp.maximum(m, jnp.where(sels[n], m_sc[n][:, 0:1], NEG_BIG))
        w_o = jnp.exp(m_o - m)
        l = l_o * w_o
        acc = a_o * w_o
        for n in range(nb):
            w_n = jnp.where(sels[n], jnp.exp(m_sc[n][:, 0:1] - m), 0.0)
            l = l + w_n * l_sc[n][:, 0:1]
            acc = acc + w_n * acc_sc[n]
        o = acc / l
        rowhead = lax.broadcasted_iota(jnp.int32, (rows, 1), 0) // tdec
        colhead = lax.broadcasted_iota(jnp.int32, (1, o.shape[1]), 1) // MB_HEAD_DIM
        o = jnp.where(rowhead == colhead, o, 0.0)
        o_ref[...] = jnp.sum(o.reshape(MB_HEADS, tdec, o.shape[1]), axis=0)


def _moba_dec(q, kn, vn, cache_k, cache_v, page_table, layer):
    ns, w = q.shape
    bs, n_pages = page_table.shape
    tdec = ns // bs
    page = cache_k.shape[2]
    ppb = MB_BLOCK // page
    nb = n_pages // ppb
    rows = MB_HEADS * tdec
    row_spec = pl.BlockSpec((tdec, w), lambda b, p, pt: (b, 0))
    kern = functools.partial(_moba_dec_kernel, ppb=ppb, tdec=tdec, nb=nb)
    return pl.pallas_call(
        kern,
        grid_spec=pltpu.PrefetchScalarGridSpec(
            num_scalar_prefetch=1,
            grid=(bs, nb),
            in_specs=[row_spec, row_spec, row_spec]
            + [_page_spec(layer, ppb, r, page, w) for r in range(ppb)] * 2,
            out_specs=row_spec,
            scratch_shapes=[pltpu.VMEM((rows, w), F32), pltpu.VMEM((nb, rows, LANES), F32),
                            pltpu.VMEM((nb, rows, LANES), F32), pltpu.VMEM((nb, rows, LANES), F32),
                            pltpu.VMEM((nb, rows, w), F32)],
        ),
        out_shape=jax.ShapeDtypeStruct((ns, w), F32),
        compiler_params=_cparams(("arbitrary", "arbitrary")),
        name="moba_dec",
    )(page_table, q, kn, vn, *([cache_k] * ppb), *([cache_v] * ppb))


def _mix_kernel(oa_ref, ob_ref, sg_ref, x_ref, gt_ref, sh2_ref, sc2_ref, g_ref, b_ref,
                wa_ref, wb_ref, wo_ref, x1_ref, h2_ref, h2b_ref, *, alpha):
    ya = _dot(oa_ref[...].astype(BF16), wa_ref[...])
    yb = _dot(ob_ref[...].astype(BF16), wb_ref[...])
    merged = sg_ref[:, :D_MODEL] * ya + sg_ref[:, D_MODEL:] * yb
    y = _dot(merged.astype(BF16), wo_ref[...])
    x1 = _layer_norm(alpha * x_ref[...] + gt_ref[...] * y, g_ref[...], b_ref[...])
    x1_ref[...] = x1
    h2 = x1 * (1.0 + sc2_ref[...]) + sh2_ref[...]
    h2_ref[...] = h2
    h2b_ref[...] = h2.astype(BF16)


def _mix_out(oa, ob, sg, x, mod1, mod2, per_token, tokens_per_seq, ln_g, ln_b, wa, wb, wo, alpha, tm):
    n, d = x.shape
    tiles_per_seq = max(tokens_per_seq // tm, 1)
    row = pl.BlockSpec((tm, d), lambda i: (i, 0))
    half = pl.BlockSpec((tm, QKV_W), lambda i: (i, 0))
    vec = pl.BlockSpec((1, d), lambda i: (0, 0))
    return pl.pallas_call(
        functools.partial(_mix_kernel, alpha=alpha),
        grid=(n // tm,),
        in_specs=[half, half, pl.BlockSpec((tm, 2 * d), lambda i: (i, 0)), row,
                  _mod_spec(per_token, tm, tiles_per_seq, 2),
                  _mod_spec(per_token, tm, tiles_per_seq, 0),
                  _mod_spec(per_token, tm, tiles_per_seq, 1),
                  vec, vec,
                  pl.BlockSpec((QKV_W, d), lambda i: (0, 0)),
                  pl.BlockSpec((QKV_W, d), lambda i: (0, 0)),
                  pl.BlockSpec((d, d), lambda i: (0, 0))],
        out_specs=[row, row, row],
        out_shape=[jax.ShapeDtypeStruct((n, d), F32), jax.ShapeDtypeStruct((n, d), F32),
                   jax.ShapeDtypeStruct((n, d), BF16)],
        compiler_params=_cparams(("arbitrary",)),
        name="mix_out",
    )(oa, ob, sg, x, mod1, mod2, mod2, ln_g.reshape(1, d), ln_b.reshape(1, d), wa, wb, wo)


PEER_NEXT = PEER_TOPK + 1
EXP_CLAMP = 80.0


def _top_values(x, count):
    vals = []
    for _ in range(count):
        m = jnp.max(x, axis=0)
        vals.append(m)
        x = jnp.where(x == m[None], -jnp.inf, x)
    return vals


def _route_kernel(h2_ref, wqh_ref, wql_ref, l1h_ref, l1l_ref, l2h_ref, l2l_ref, l2ph_ref, l2pl_ref,
                  thr_ref, e1_ref, e2_ref):
    hh, hl = _split(h2_ref[...])
    q = _dot3(hh, hl, wqh_ref[...], wql_ref[...])
    qh, ql = _split(q)
    tm = q.shape[0]
    nk, nh = PEER_NKEYS, PEER_HEADS
    s1 = _dot3_nt(l1h_ref[...], l1l_ref[...], qh, ql).reshape(nk, nh, tm)
    s2 = _dot3_nt(l2h_ref[...], l2l_ref[...], qh, ql).reshape(nk, nh, tm)
    s2p = _dot3_nt(l2ph_ref[...], l2pl_ref[...], qh, ql).reshape(nh, nk, tm)
    a = _top_values(s1, PEER_NEXT)
    b = _top_values(s2, PEER_NEXT)
    cands = [a[i] + b[j] for i in range(PEER_NEXT) for j in range(PEER_NEXT) if (i + 1) * (j + 1) <= PEER_NEXT]
    x = jnp.stack(cands, axis=0)
    t = _top_values(x, PEER_NEXT)
    tau = 0.5 * (t[PEER_TOPK - 1] + t[PEER_TOPK])
    m0 = a[0] + b[0]
    z = jnp.zeros_like(m0)
    for cnd in cands:
        z = z + jnp.where(cnd > tau, jnp.exp(cnd - m0), 0.0)
    thr_ref[...] = jnp.exp(jnp.minimum(tau[None] - s1 - b[0][None], EXP_CLAMP))
    e1_ref[...] = jnp.exp(s1 - a[0][None]) / z[None]
    e2_ref[...] = jnp.exp(s2p - jnp.max(s2p, axis=1, keepdims=True))


def _peer_route(h2, wqh, wql, lmats, tm):
    n, d = h2.shape
    nk, nh = PEER_NKEYS, PEER_HEADS
    wspec = pl.BlockSpec((d, d), lambda i: (0, 0))
    return pl.pallas_call(
        _route_kernel,
        grid=(n // tm,),
        in_specs=[pl.BlockSpec((tm, d), lambda i: (i, 0))] + [wspec] * 8,
        out_specs=[pl.BlockSpec((nk, nh, tm), lambda i: (0, 0, i)),
                   pl.BlockSpec((nk, nh, tm), lambda i: (0, 0, i)),
                   pl.BlockSpec((nh, nk, tm), lambda i: (0, 0, i))],
        out_shape=[jax.ShapeDtypeStruct((nk, nh, n), F32), jax.ShapeDtypeStruct((nk, nh, n), F32),
                   jax.ShapeDtypeStruct((nh, nk, n), F32)],
        compiler_params=_cparams(("arbitrary",)),
        name="peer_route",
    )(h2, wqh, wql, *lmats)


def _peer_lmats(keys):
    nh, _, nk, hd = keys.shape
    eye = jnp.eye(nh, dtype=F32)

    def blockdiag(c):
        sel = jnp.zeros((2,), F32).at[c].set(1.0)
        full = (keys[:, c][:, :, None, None, :] * eye[:, None, :, None, None]
                * sel[None, None, None, :, None])
        return full

    out = []
    for c, interleave in ((0, True), (1, True), (1, False)):
        full = blockdiag(c)
        if interleave:
            full = jnp.transpose(full, (1, 0, 2, 3, 4))
        m = full.reshape(nh * nk, nh * 2 * hd)
        out.extend(_split(m))
    return out


def _gelu(x):
    return 0.5 * x * (1.0 + lax.erf(x * (2.0 ** -0.5)))


def _peer_kernel(h2b_ref, u_ref, vt_ref, thr_ref, e1_ref, e2_ref, x1_ref, gt_ref, g_ref, b_ref,
                 o_ref, acc_sc, *, gs, alpha):
    j = pl.program_id(1)
    nk, nh = PEER_NKEYS, PEER_HEADS
    tm = h2b_ref.shape[0]

    @pl.when(j == 0)
    def _init():
        acc_sc[...] = jnp.zeros(acc_sc.shape, F32)

    h2b = h2b_ref[...]
    parts = []
    for gg in range(gs):
        g = j * gs + gg
        at = _dot_nt(u_ref[gg * nk:(gg + 1) * nk, :], h2b)
        thr8 = thr_ref[g]
        e18 = e1_ref[g]
        w = jnp.zeros((nk, tm), F32)
        for h in range(nh):
            e2 = e2_ref[h]
            w = w + jnp.where(e2 > thr8[h:h + 1, :], e2, 0.0) * e18[h:h + 1, :]
        parts.append((w * _gelu(at)).astype(BF16))
    p = jnp.concatenate(parts, axis=0)
    acc_sc[...] += _dot(vt_ref[...], p)

    @pl.when(j == pl.num_programs(1) - 1)
    def _finish():
        f = acc_sc[...].T
        o_ref[...] = _layer_norm(alpha * x1_ref[...] + gt_ref[...] * f, g_ref[...], b_ref[...])


def _peer_dense(h2b, u_bf, vt_bf, thr, e1, e2, x1, mod2, per_token, tokens_per_seq, ln_g, ln_b, alpha, tm, gs):
    n, d = x1.shape
    nk, nh = PEER_NKEYS, PEER_HEADS
    tiles_per_seq = max(tokens_per_seq // tm, 1)
    if per_token:
        gt_spec = pl.BlockSpec((tm, d), lambda i, j: (i, 2))
    else:
        gt_spec = pl.BlockSpec((None, 1, d), lambda i, j: (i // tiles_per_seq, 0, 2))
    return pl.pallas_call(
        functools.partial(_peer_kernel, gs=gs, alpha=alpha),
        grid=(n // tm, nk // gs),
        in_specs=[pl.BlockSpec((tm, d), lambda i, j: (i, 0)),
                  pl.BlockSpec((gs * nk, d), lambda i, j: (j, 0)),
                  pl.BlockSpec((d, gs * nk), lambda i, j: (0, j)),
                  pl.BlockSpec((nk, nh, tm), lambda i, j: (0, 0, i)),
                  pl.BlockSpec((nk, nh, tm), lambda i, j: (0, 0, i)),
                  pl.BlockSpec((nh, nk, tm), lambda i, j: (0, 0, i)),
                  pl.BlockSpec((tm, d), lambda i, j: (i, 0)),
                  gt_spec,
                  pl.BlockSpec((1, d), lambda i, j: (0, 0)),
                  pl.BlockSpec((1, d), lambda i, j: (0, 0))],
        out_specs=pl.BlockSpec((tm, d), lambda i, j: (i, 0)),
        out_shape=jax.ShapeDtypeStruct((n, d), F32),
        scratch_shapes=[pltpu.VMEM((d, tm), F32)],
        compiler_params=_cparams(("arbitrary", "arbitrary")),
        name="peer_dense",
    )(h2b, u_bf, vt_bf, thr, e1, e2, x1, mod2, ln_g.reshape(1, d), ln_b.reshape(1, d))


def _pick_tile(n, pref):
    t = pref
    while n % t:
        t //= 2
    return t


def kernel(x_prompt, x_sample, cache_dk, cache_dv, cache_mk, cache_mv, page_table, c_prompt, c_sample,
           ln_in_g, ln_in_b, w_ada_mix, b_ada_mix, w_in, diff_lambda, diff_subln_g, w_branch_a, w_branch_b,
           w_out, ln1_g, ln1_b, w_ada_ffn, b_ada_ffn, peer_wq, peer_keys, peer_u, peer_v, ln2_g, ln2_b):
    batch, seq, d = x_prompt.shape
    dec_batch, dec_seq, _ = x_sample.shape
    depth = w_in.shape[0]
    n_pool, _, page, _, _ = cache_dk.shape
    past_len = page_table.shape[1] * page
    alpha = (2 * depth) ** 0.25
    assert seq % MB_BLOCK == 0 and past_len % MB_BLOCK == 0 and MB_BLOCK % page == 0
    assert dec_seq <= MB_BLOCK and dec_seq % SUBLANES == 0

    w_in_bf = w_in.astype(BF16)
    wa_bf = w_branch_a.astype(BF16)
    wb_bf = w_branch_b.astype(BF16)
    wo_bf = w_out.astype(BF16)
    u_bf = peer_u.astype(BF16)
    vt_bf = jnp.swapaxes(peer_v, 1, 2).astype(BF16)
    wq_split = [_split(peer_wq[l]) for l in range(depth)]
    lmats = [_peer_lmats(peer_keys[l]) for l in range(depth)]
    cdk = cache_dk.reshape(n_pool, depth, page, QKV_W)
    cdv = cache_dv.reshape(n_pool, depth, page, QKV_W)
    cmk = cache_mk.reshape(n_pool, depth, page, QKV_W)
    cmv = cache_mv.reshape(n_pool, depth, page, QKV_W)
    g4 = jnp.tile(diff_subln_g, (1, DA_HEADS))

    r = batch + dec_batch
    r_pad = -(-r // SUBLANES) * SUBLANES
    c_all = jnp.concatenate([c_prompt, c_sample, jnp.zeros((r_pad - r, d), F32)], axis=0)
    mods_mix = _ada(c_all, w_ada_mix, b_ada_mix)
    mods_ffn = _ada(c_all, w_ada_ffn, b_ada_ffn)

    def run_group(x, row0, nseq, tlen, pos, is_prompt):
        n = nseq * tlen
        per_token = not is_prompt
        tm = _pick_tile(n, 512)
        if is_prompt:
            tm = min(tm, _pick_tile(tlen, 512))
            tabs = _rope_tables(pos)
        else:
            tabs = _rope_tables(jnp.tile(pos, tm // tlen))
        xs = _ln_rows(x.reshape(n, d), ln_in_g, ln_in_b, tm)
        rows = {k: [] for k in ("dk", "dv", "mk", "mv")}
        for l in range(depth):
            lam_init = 0.8 - 0.6 * math.exp(-0.3 * l)
            mod1 = _mod_operand(mods_mix[l, row0:row0 + nseq], per_token, tlen)
            mod2 = _mod_operand(mods_ffn[l, row0:row0 + nseq], per_token, tlen)
            (qa, ka, va, qb, kb, vb, qab, kab, vab, qbb, kbb, vbb, sg) = _inproj(
                xs, mod1, per_token, tlen, w_in_bf[l], tabs, tm)
            if is_prompt:
                tq = _pick_tile(tlen, 512)
                oa = _attn_prompt("diff", qab, kab, vab,
                                  (diff_lambda[l], diff_subln_g[l].reshape(1, LANES)),
                                  nseq, tlen, tq, tq, lam_init)
                km = _kmean(kb)
                ob = _attn_prompt("moba", qbb, kbb, vbb, (qb, km), nseq, tlen, tq, MB_BLOCK)
            else:
                oa = _diff_dec(qa, ka, va, diff_lambda[l], g4[l].reshape(1, QKV_W), cdk, cdv, page_table, l,
                               lam_init, 4 if page_table.shape[1] % 4 == 0 else 1)
                ob = _moba_dec(qb, kb, vb, cmk, cmv, page_table, l)
            x1, h2, h2b = _mix_out(oa, ob, sg, xs, mod1, mod2, per_token, tlen, ln1_g[l], ln1_b[l],
                                   wa_bf[l], wb_bf[l], wo_bf[l], alpha, tm)
            thr, e1, e2 = _peer_route(h2, wq_split[l][0], wq_split[l][1], lmats[l], _pick_tile(n, 256))
            xs = _peer_dense(h2b, u_bf[l], vt_bf[l], thr, e1, e2, x1, mod2, per_token, tlen,
                             ln2_g[l], ln2_b[l], alpha, tm, 4)
            rows["dk"].append(ka)
            rows["dv"].append(va)
            rows["mk"].append(kb)
            rows["mv"].append(vb)
        y = xs.reshape(nseq, tlen, d)
        dk = jnp.stack(rows["dk"], 0).reshape(depth, nseq, tlen, DA_HEADS, 2 * DA_HEAD_DIM).swapaxes(0, 1)
        dv = jnp.stack(rows["dv"], 0).reshape(depth, nseq, tlen, DA_HEADS, 2 * DA_HEAD_DIM).swapaxes(0, 1)
        mk = jnp.stack(rows["mk"], 0).reshape(depth, nseq, tlen, MB_HEADS, MB_HEAD_DIM).swapaxes(0, 1)
        mv = jnp.stack(rows["mv"], 0).reshape(depth, nseq, tlen, MB_HEADS, MB_HEAD_DIM).swapaxes(0, 1)
        return y, dk, dv, mk, mv

    yp, dkp, dvp, mkp, mvp = run_group(x_prompt, 0, batch, seq, jnp.arange(seq), True)
    ys, dks, dvs, mks, mvs = run_group(x_sample, batch, dec_batch, dec_seq, past_len + jnp.arange(dec_seq), False)
    return (yp, ys, dkp, dvp, mkp, mvp, dks, dvs, mks, mvs)
```

```python
import functools
import math

import numpy as np
import jax
import jax.numpy as jnp
from jax import lax
from jax.experimental import pallas as pl
from jax.experimental.pallas import tpu as pltpu

D_MODEL = 1024
DA_HEADS = 4
DA_HEAD_DIM = 64
MB_HEADS = 8
MB_HEAD_DIM = 64
MB_BLOCK = 256
MB_TOPK = 3
ROPE_THETA = 500000.0
ROPE_DIM = 16
PEER_HEADS = 8
PEER_NKEYS = 128
PEER_QDIM = 128
PEER_TOPK = 16
LN_EPS = 1e-5
QKV_W = 512
HEAD_CHUNK = 64

LANES = 128
SUBLANES = 8
VMEM_LIMIT_BYTES = 52 * 1024 * 1024

F32 = jnp.float32
BF16 = jnp.bfloat16
NEG_BIG = -1e30


def _cparams(sem):
    return pltpu.CompilerParams(dimension_semantics=sem, vmem_limit_bytes=VMEM_LIMIT_BYTES)


def _dot(a, b):
    return jnp.dot(a, b, preferred_element_type=F32)


def _dot_nt(a, b):
    return lax.dot_general(a, b, (((1,), (1,)), ((), ())), preferred_element_type=F32)


def _split(x):
    hi = x.astype(BF16)
    lo = (x - hi.astype(F32)).astype(BF16)
    return hi, lo


def _dot3(ah, al, bh, bl):
    return _dot(ah, bh) + (_dot(ah, bl) + _dot(al, bh))


def _dot3_nt(ah, al, bh, bl):
    return _dot_nt(ah, bh) + (_dot_nt(ah, bl) + _dot_nt(al, bh))


def _layer_norm(x, g, b):
    mu = jnp.mean(x, axis=-1, keepdims=True)
    xc = x - mu
    var = jnp.mean(xc * xc, axis=-1, keepdims=True)
    return xc * lax.rsqrt(var + LN_EPS) * g + b


def _ada_kernel(c_ref, w_ref, b_ref, o_ref):
    c = c_ref[...]
    cs = c * (1.0 / (1.0 + jnp.exp(-c)))
    ch, cl = _split(cs)
    wh, wl = _split(w_ref[...])
    o_ref[...] = _dot3(ch, cl, wh, wl) + b_ref[...]


def _ada(c_all, w, b):
    depth, d, d3 = w.shape
    r = c_all.shape[0]
    tn = 1024
    return pl.pallas_call(
        _ada_kernel,
        grid=(depth, d3 // tn),
        in_specs=[
            pl.BlockSpec((r, d), lambda l, j: (0, 0)),
            pl.BlockSpec((None, d, tn), lambda l, j: (l, 0, j)),
            pl.BlockSpec((None, 1, tn), lambda l, j: (l, 0, j)),
        ],
        out_specs=pl.BlockSpec((None, r, tn), lambda l, j: (l, 0, j)),
        out_shape=jax.ShapeDtypeStruct((depth, r, d3), F32),
        compiler_params=_cparams(("arbitrary", "arbitrary")),
        name="ada",
    )(c_all, w, b.reshape(depth, 1, d3))


def _ln_kernel(x_ref, g_ref, b_ref, o_ref):
    o_ref[...] = _layer_norm(x_ref[...], g_ref[...], b_ref[...])


def _ln_rows(x, g, b, tm):
    n, d = x.shape
    return pl.pallas_call(
        _ln_kernel,
        grid=(n // tm,),
        in_specs=[
            pl.BlockSpec((tm, d), lambda i: (i, 0)),
            pl.BlockSpec((1, d), lambda i: (0, 0)),
            pl.BlockSpec((1, d), lambda i: (0, 0)),
        ],
        out_specs=pl.BlockSpec((tm, d), lambda i: (i, 0)),
        out_shape=jax.ShapeDtypeStruct((n, d), F32),
        compiler_params=_cparams(("arbitrary",)),
        name="ln_in",
    )(x, g.reshape(1, d), b.reshape(1, d))


def _mod_operand(mods, per_token, tokens_per_seq):
    if per_token:
        return jnp.repeat(mods, tokens_per_seq, axis=0)
    return mods.reshape(mods.shape[0], 1, mods.shape[1])


def _mod_spec(per_token, tm, tiles_per_seq, part):
    if per_token:
        return pl.BlockSpec((tm, D_MODEL), lambda i: (i, part))
    return pl.BlockSpec((None, 1, D_MODEL), lambda i: (i // tiles_per_seq, 0, part))


def _rope(z, c, s1, s2):
    reps = z.shape[1] // LANES
    c = jnp.tile(c, (1, reps))
    s1 = jnp.tile(s1, (1, reps))
    s2 = jnp.tile(s2, (1, reps))
    half = ROPE_DIM // 2
    return z * c + pltpu.roll(z, half, 1) * s1 + pltpu.roll(z, z.shape[1] - half, 1) * s2


def _inproj_kernel(x_ref, sh_ref, sc_ref, w_ref, wvt_ref, c_ref, s1_ref, s2_ref,
                   qa_ref, ka_ref, va_ref, qb_ref, kb_ref, vb_ref,
                   qab_ref, kab_ref, vat_ref, qbb_ref, kbb_ref, vbt_ref, sg_ref):
    h = x_ref[...] * (1.0 + sc_ref[...]) + sh_ref[...]
    hb = h.astype(BF16)
    c = c_ref[...]
    s1 = s1_ref[...]
    s2 = s2_ref[...]
    f32_outs = (qa_ref, ka_ref, va_ref, qb_ref, kb_ref, vb_ref)
    bf_outs = (qab_ref, kab_ref, None, qbb_ref, kbb_ref, None)
    rotate = (True, True, False, True, True, False)
    for j in range(6):
        z = _dot(hb, w_ref[:, j * QKV_W:(j + 1) * QKV_W])
        if rotate[j]:
            z = _rope(z, c, s1, s2)
        f32_outs[j][...] = z
        if bf_outs[j] is not None:
            bf_outs[j][...] = z.astype(BF16)
    vat_ref[...] = _dot_nt(wvt_ref[0], hb).astype(BF16)
    vbt_ref[...] = _dot_nt(wvt_ref[1], hb).astype(BF16)
    for j in range(4):
        z = _dot(hb, w_ref[:, (6 + j) * QKV_W:(7 + j) * QKV_W])
        sg_ref[:, j * QKV_W:(j + 1) * QKV_W] = 1.0 / (1.0 + jnp.exp(-z))


def _inproj(x, mod, per_token, tokens_per_seq, w_bf, wvt_bf, rope_tabs, tm):
    n, d = x.shape
    in_w = w_bf.shape[1]
    tiles_per_seq = max(tokens_per_seq // tm, 1)
    ctab, s1tab, s2tab = rope_tabs
    n_tab = ctab.shape[0] // tm
    slab = pl.BlockSpec((tm, QKV_W), lambda i: (i, 0))
    slab_t = pl.BlockSpec((QKV_W, tm), lambda i: (0, i))
    tab = pl.BlockSpec((tm, LANES), lambda i: (i % n_tab, 0))
    row_f32 = jax.ShapeDtypeStruct((n, QKV_W), F32)
    row_bf = jax.ShapeDtypeStruct((n, QKV_W), BF16)
    col_bf = jax.ShapeDtypeStruct((QKV_W, n), BF16)
    outs = pl.pallas_call(
        _inproj_kernel,
        grid=(n // tm,),
        in_specs=[
            pl.BlockSpec((tm, d), lambda i: (i, 0)),
            _mod_spec(per_token, tm, tiles_per_seq, 0),
            _mod_spec(per_token, tm, tiles_per_seq, 1),
            pl.BlockSpec((d, in_w), lambda i: (0, 0)),
            pl.BlockSpec((2, QKV_W, d), lambda i: (0, 0, 0)),
            tab, tab, tab,
        ],
        out_specs=[slab] * 6 + [slab, slab, slab_t, slab, slab, slab_t]
        + [pl.BlockSpec((tm, 2 * D_MODEL), lambda i: (i, 0))],
        out_shape=[row_f32] * 6 + [row_bf, row_bf, col_bf, row_bf, row_bf, col_bf]
        + [jax.ShapeDtypeStruct((n, 2 * D_MODEL), F32)],
        compiler_params=_cparams(("arbitrary",)),
        name="inproj",
    )(x, mod, mod, w_bf, wvt_bf, ctab, s1tab, s2tab)
    return outs


def _rope_tables(pos):
    half = ROPE_DIM // 2
    inv_freq = ROPE_THETA ** (-jnp.arange(half, dtype=F32) / half)
    ang = pos.astype(F32)[:, None] * inv_freq[None, :]
    cos = jnp.cos(ang)
    sin = jnp.sin(ang)
    t = pos.shape[0]
    ones = jnp.ones((t, HEAD_CHUNK - ROPE_DIM), F32)
    zeros = jnp.zeros((t, HEAD_CHUNK - ROPE_DIM), F32)
    zh = jnp.zeros((t, half), F32)
    c = jnp.concatenate([cos, cos, ones], axis=1)
    s1 = jnp.concatenate([zh, sin, zeros], axis=1)
    s2 = jnp.concatenate([-sin, zh, zeros], axis=1)
    reps = LANES // HEAD_CHUNK
    return tuple(jnp.tile(a, (1, reps)) for a in (c, s1, s2))


def _kmean_kernel(k_ref, o_ref):
    k = k_ref[...]
    nb = k.shape[0] // MB_BLOCK
    o_ref[...] = jnp.sum(k.reshape(nb, MB_BLOCK, k.shape[1]), axis=1) * (1.0 / MB_BLOCK)


def _kmean(kb):
    n, w = kb.shape
    rows = SUBLANES * MB_BLOCK
    return pl.pallas_call(
        _kmean_kernel,
        grid=(n // rows,),
        in_specs=[pl.BlockSpec((rows, w), lambda i: (i, 0))],
        out_specs=pl.BlockSpec((SUBLANES, w), lambda i: (i, 0)),
        out_shape=jax.ShapeDtypeStruct((n // MB_BLOCK, w), F32),
        compiler_params=_cparams(("arbitrary",)),
        name="moba_kmean",
    )(kb)


def _moba_select(gate, own, nblk):
    nidx = lax.broadcasted_iota(jnp.int32, (nblk, 1), 0)
    rank = jnp.zeros(gate.shape, jnp.int32)
    for n2 in range(nblk):
        row = gate[n2:n2 + 1, :]
        beats = (row > gate) | ((row == gate) & (n2 < nidx))
        rank = rank + jnp.where(beats & (n2 < own), 1, 0)
    selected = (nidx == own) | ((nidx < own) & (rank < MB_TOPK))
    return jnp.where(selected, 0.0, NEG_BIG)


def _attn_kernel(qt_ref, kt_ref, *refs, mode, tq, tk, nblk, lam_init):
    if mode == "diff":
        q_ref, k_ref, vt_ref, lam_ref, g_ref, o_ref, qc_sc, m_sc, l_sc, acc_sc = refs
    else:
        q_ref, k_ref, vt_ref, qf_ref, km_ref, o_ref, qc_sc, m_sc, l_sc, acc_sc, sel_sc = refs
    step = pl.program_id(2)
    qi = qt_ref[step]
    ki = kt_ref[step]
    lane = lax.broadcasted_iota(jnp.int32, (1, LANES), 1)
    lo = lane < HEAD_CHUNK
    qpos = qi * tq + lax.broadcasted_iota(jnp.int32, (1, tq), 1)
    kpos = ki * tk + lax.broadcasted_iota(jnp.int32, (tk, 1), 0)
    vrows = acc_sc.shape[1]

    @pl.when(ki == 0)
    def _init():
        q = q_ref[...]
        scale = DA_HEAD_DIM ** -0.5
        qc_sc[0] = jnp.where(lo, q, jnp.zeros_like(q)) * scale
        qc_sc[1] = jnp.where(lo, jnp.zeros_like(q), q) * scale
        m_sc[...] = jnp.full(m_sc.shape, NEG_BIG, F32)
        l_sc[...] = jnp.zeros(l_sc.shape, F32)
        acc_sc[...] = jnp.zeros(acc_sc.shape, F32)
        if mode == "moba":
            qh, ql = _split(qf_ref[...])
            km = km_ref[...]
            own = qpos // MB_BLOCK
            for c in range(2):
                kh, kl = _split(jnp.where(lo if c == 0 else ~lo, km, 0.0))
                gate = _dot3_nt(kh, kl, qh, ql)
                sel_sc[c] = _moba_select(gate, own, nblk)

    k = k_ref[...]
    vt = vt_ref[...]
    causal = kpos <= qpos
    for c in range(2):
        s = _dot_nt(k, qc_sc[c])
        if mode == "moba":
            s = s + sel_sc[c, pl.ds(ki * (tk // MB_BLOCK), 1), :]
        s = jnp.where(causal, s, NEG_BIG)
        m_prev = m_sc[c]
        m_new = jnp.maximum(m_prev, jnp.max(s, axis=0, keepdims=True))
        alpha = jnp.exp(m_prev - m_new)
        p = jnp.exp(s - m_new)
        l_sc[c] = alpha * l_sc[c] + jnp.sum(p, axis=0, keepdims=True)
        vc = vt if mode == "diff" else vt[c * vrows:(c + 1) * vrows, :]
        acc_sc[c] = alpha * acc_sc[c] + _dot(vc, p.astype(BF16))
        m_sc[c] = m_new

    last_ki = ((qi + 1) * tq - 1) // tk

    @pl.when(ki == last_ki)
    def _finish():
        o0 = acc_sc[0] / l_sc[0]
        o1 = acc_sc[1] / l_sc[1]
        if mode == "diff":
            lp = lam_ref[...]
            lam = (jnp.exp(jnp.sum(lp[0:1] * lp[1:2], axis=1, keepdims=True))
                   - jnp.exp(jnp.sum(lp[2:3] * lp[3:4], axis=1, keepdims=True)) + lam_init)
            o = o0 - lam * o1
            o = o * lax.rsqrt(jnp.mean(o * o, axis=0, keepdims=True) + LN_EPS) * g_ref[...] * (1.0 - lam_init)
        else:
            o = jnp.concatenate([o0, o1], axis=0)
        o_ref[...] = o.T.astype(o_ref.dtype)


def _tri_steps(t, tq, tk):
    qs, ks = [], []
    for qi in range(t // tq):
        for ki in range(((qi + 1) * tq - 1) // tk + 1):
            qs.append(qi)
            ks.append(ki)
    return np.asarray(qs, np.int32), np.asarray(ks, np.int32)


def _attn_prompt(mode, q, k, vt, extra, batch, t, tq, tk, lam_init=0.0):
    n, w = q.shape
    nslab = w // LANES
    nblk = t // MB_BLOCK
    qs, ks = _tri_steps(t, tq, tk)
    nq, nk = t // tq, t // tk
    qspec = pl.BlockSpec((tq, LANES), lambda b, h, s, qt, kt: (b * nq + qt[s], h))
    kspec = pl.BlockSpec((tk, LANES), lambda b, h, s, qt, kt: (b * nk + kt[s], h))
    vspec = pl.BlockSpec((LANES, tk), lambda b, h, s, qt, kt: (h, b * nk + kt[s]))
    if mode == "diff":
        extra_specs = [pl.BlockSpec((4, DA_HEAD_DIM), lambda b, h, s, qt, kt: (0, 0)),
                       pl.BlockSpec((LANES, 1), lambda b, h, s, qt, kt: (0, 0))]
        scratch = []
        vrows = LANES
    else:
        assert tk == MB_BLOCK
        extra_specs = [qspec, pl.BlockSpec((nblk, LANES), lambda b, h, s, qt, kt: (b, h))]
        scratch = [pltpu.VMEM((2, nblk, tq), F32)]
        vrows = HEAD_CHUNK
    kern = functools.partial(_attn_kernel, mode=mode, tq=tq, tk=tk, nblk=nblk, lam_init=lam_init)
    return pl.pallas_call(
        kern,
        grid_spec=pltpu.PrefetchScalarGridSpec(
            num_scalar_prefetch=2,
            grid=(batch, nslab, len(qs)),
            in_specs=[qspec, kspec, vspec] + extra_specs,
            out_specs=qspec,
            scratch_shapes=[pltpu.VMEM((2, tq, LANES), BF16),
                            pltpu.VMEM((2, 1, tq), F32), pltpu.VMEM((2, 1, tq), F32),
                            pltpu.VMEM((2, vrows, tq), F32)] + scratch,
        ),
        out_shape=jax.ShapeDtypeStruct((n, w), BF16),
        compiler_params=_cparams(("arbitrary", "arbitrary", "arbitrary")),
        name="attn_" + mode,
    )(jnp.asarray(qs), jnp.asarray(ks), q, k, vt, *extra)


def _stack_heads(q8, n_groups):
    t, w = q8.shape
    rows = n_groups * t
    qt = jnp.concatenate([q8] * n_groups, axis=0)
    rowgrp = lax.broadcasted_iota(jnp.int32, (rows, 1), 0) // t
    colgrp = lax.broadcasted_iota(jnp.int32, (1, w), 1) // (w // n_groups)
    diag = rowgrp == colgrp
    return jnp.where(diag, qt, 0.0), diag


def _softmax_update(s_list, v_list, m_prev, l_prev, acc_prev):
    m_cur = functools.reduce(jnp.maximum, [jnp.max(s, axis=1, keepdims=True) for s in s_list])
    m_new = jnp.maximum(m_prev, m_cur)
    alpha = jnp.exp(m_prev - m_new)
    l = alpha * l_prev
    acc = alpha * acc_prev
    for s, v in zip(s_list, v_list):
        p = jnp.exp(s - m_new)
        l = l + jnp.sum(p, axis=1, keepdims=True)
        acc = acc + _dot(p.astype(v.dtype), v)
    return m_new, l, acc


def _diff_dec_kernel(pt_ref, q_ref, kn_ref, vn_ref, lam_ref, g_ref, *refs, pp, tdec, lam_init):
    k_refs = refs[:pp]
    v_refs = refs[pp:2 * pp]
    o_ref, q_sc, m_sc, l_sc, acc_sc = refs[2 * pp:]
    p_idx = pl.program_id(1)
    nh = DA_HEADS
    rows = 2 * nh * tdec
    lane = lax.broadcasted_iota(jnp.int32, (1, LANES), 1)
    lo = lane < HEAD_CHUNK
    rowhead = lax.broadcasted_iota(jnp.int32, (rows, 1), 0) // (2 * tdec)

    @pl.when(p_idx == 0)
    def _init():
        q = q_ref[...] * (DA_HEAD_DIM ** -0.5)
        parts = []
        for h in range(nh):
            qh = q[:, h * LANES:(h + 1) * LANES]
            parts.append(jnp.where(lo, qh, 0.0))
            parts.append(jnp.where(lo, 0.0, qh))
        q_sc[...] = jnp.concatenate(parts, axis=0)
        m_sc[...] = jnp.full(m_sc.shape, NEG_BIG, F32)
        l_sc[...] = jnp.zeros(l_sc.shape, F32)
        acc_sc[...] = jnp.zeros(acc_sc.shape, F32)

    qs = q_sc[...]
    qb = qs.astype(BF16)
    page_rows = k_refs[0].shape[0]
    same = rowhead == lax.broadcasted_iota(jnp.int32, (1, page_rows), 1) % nh
    s_list = [jnp.where(same, _dot_nt(qb, k_refs[r][...].astype(BF16)), NEG_BIG) for r in range(pp)]
    v_list = [v_refs[r][...].astype(BF16) for r in range(pp)]
    m_new, l_new, acc_new = _softmax_update(s_list, v_list, m_sc[...], l_sc[...], acc_sc[...])
    m_sc[...] = m_new
    l_sc[...] = l_new
    acc_sc[...] = acc_new

    @pl.when(p_idx == pl.num_programs(1) - 1)
    def _finish():
        kn = kn_ref[...]
        vn = vn_ref[...]
        knr = jnp.concatenate([kn[:, h * LANES:(h + 1) * LANES] for h in range(nh)], axis=0)
        vnr = jnp.concatenate([vn[:, h * LANES:(h + 1) * LANES] for h in range(nh)], axis=0)
        col = lax.broadcasted_iota(jnp.int32, (1, nh * tdec), 1)
        trow = lax.broadcasted_iota(jnp.int32, (rows, 1), 0) % tdec
        valid = (rowhead == col // tdec) & (col % tdec <= trow)
        s = jnp.where(valid, _dot_nt(qs, knr), NEG_BIG)
        _, l, acc = _softmax_update([s], [vnr], m_sc[...], l_sc[...], acc_sc[...])
        o = acc / l
        lp = lam_ref[...]
        lam = (jnp.exp(jnp.sum(lp[0:1] * lp[1:2], axis=1, keepdims=True))
               - jnp.exp(jnp.sum(lp[2:3] * lp[3:4], axis=1, keepdims=True)) + lam_init)
        outs = []
        for h in range(nh):
            oh = o[2 * h * tdec:(2 * h + 1) * tdec] - lam * o[(2 * h + 1) * tdec:(2 * h + 2) * tdec]
            outs.append(oh * lax.rsqrt(jnp.mean(oh * oh, axis=1, keepdims=True) + LN_EPS))
        o_ref[...] = jnp.concatenate(outs, axis=1) * g_ref[...] * (1.0 - lam_init)


def _page_spec(layer, pp, r, rows, w):
    return pl.BlockSpec((None, None, rows, w), lambda b, p, pt: (pt[b, p * pp + r], layer, 0, 0))


def _diff_dec(q, kn, vn, lam_p, g4, cache_k, cache_v, page_table, layer, lam_init, pp):
    ns, w = q.shape
    bs, n_pages = page_table.shape
    tdec = ns // bs
    page_rows = cache_k.shape[2]
    rows = 2 * DA_HEADS * tdec
    row_spec = pl.BlockSpec((tdec, w), lambda b, p, pt: (b, 0))
    kern = functools.partial(_diff_dec_kernel, pp=pp, tdec=tdec, lam_init=lam_init)
    return pl.pallas_call(
        kern,
        grid_spec=pltpu.PrefetchScalarGridSpec(
            num_scalar_prefetch=1,
            grid=(bs, n_pages // pp),
            in_specs=[row_spec, row_spec, row_spec,
                      pl.BlockSpec((4, DA_HEAD_DIM), lambda b, p, pt: (0, 0)),
                      pl.BlockSpec((1, w), lambda b, p, pt: (0, 0))]
            + [_page_spec(layer, pp, r, page_rows, LANES) for r in range(pp)] * 2,
            out_specs=row_spec,
            scratch_shapes=[pltpu.VMEM((rows, LANES), F32), pltpu.VMEM((rows, 1), F32),
                            pltpu.VMEM((rows, 1), F32), pltpu.VMEM((rows, LANES), F32)],
        ),
        out_shape=jax.ShapeDtypeStruct((ns, w), F32),
        compiler_params=_cparams(("arbitrary", "arbitrary")),
        name="diff_dec",
    )(page_table, q, kn, vn, lam_p, g4, *([cache_k] * pp), *([cache_v] * pp))


def _moba_dec_kernel(pt_ref, q_ref, kn_ref, vn_ref, *refs, ppb, tdec, nb):
    k_refs = refs[:ppb]
    v_refs = refs[ppb:2 * ppb]
    o_ref, qbd_sc, gate_sc, m_sc, l_sc, acc_sc = refs[2 * ppb:]
    n_idx = pl.program_id(1)
    scale = MB_HEAD_DIM ** -0.5
    rows = MB_HEADS * tdec

    @pl.when(n_idx == 0)
    def _init():
        qbd, _ = _stack_heads(q_ref[...], MB_HEADS)
        qbd_sc[...] = qbd

    qbd = qbd_sc[...]
    qb = (qbd * scale).astype(BF16)
    s = jnp.concatenate([_dot(qb, k_refs[r][...].astype(BF16)) for r in range(ppb)], axis=1)
    page = s.shape[1] // ppb
    gate = jnp.sum(s, axis=1, keepdims=True)
    m_n = jnp.max(s, axis=1, keepdims=True)
    p = jnp.exp(s - m_n)
    acc = jnp.zeros((rows, acc_sc.shape[2]), F32)
    for r in range(ppb):
        acc = acc + _dot_nt(p[:, r * page:(r + 1) * page].astype(BF16), v_refs[r][...].astype(BF16))
    gate_sc[n_idx] = jnp.broadcast_to(gate, (rows, LANES))
    m_sc[n_idx] = jnp.broadcast_to(m_n, (rows, LANES))
    l_sc[n_idx] = jnp.broadcast_to(jnp.sum(p, axis=1, keepdims=True), (rows, LANES))
    acc_sc[n_idx] = acc

    @pl.when(n_idx == nb - 1)
    def _finish():
        s_o = _dot_nt(qbd * scale, kn_ref[...])
        trow = lax.broadcasted_iota(jnp.int32, (rows, 1), 0) % tdec
        jcol = lax.broadcasted_iota(jnp.int32, (1, tdec), 1)
        s_o = jnp.where(jcol <= trow, s_o, NEG_BIG)
        m_o = jnp.max(s_o, axis=1, keepdims=True)
        p_o = jnp.exp(s_o - m_o)
        l_o = jnp.sum(p_o, axis=1, keepdims=True)
        a_o = _dot(p_o, vn_ref[...])
        gates = [gate_sc[n][:, 0:1] for n in range(nb)]
        sels = []
        for n in range(nb):
            rank = jnp.zeros((rows, 1), jnp.int32)
            for n2 in range(nb):
                if n2 == n:
                    continue
                beats = (gates[n2] > gates[n]) | ((gates[n2] == gates[n]) & (n2 < n))
                rank = rank + jnp.where(beats, 1, 0)
            sels.append(rank < MB_TOPK)
        m = m_o
        for n in range(nb):
            m = jnp.maximum(m, jnp.where(sels[n], m_sc[n][:, 0:1], NEG_BIG))
        w_o = jnp.exp(m_o - m)
        l = l_o * w_o
        acc = a_o * w_o
        for n in range(nb):
            w_n = jnp.where(sels[n], jnp.exp(m_sc[n][:, 0:1] - m), 0.0)
            l = l + w_n * l_sc[n][:, 0:1]
            acc = acc + w_n * acc_sc[n]
        o = acc / l
        rowhead = lax.broadcasted_iota(jnp.int32, (rows, 1), 0) // tdec
        colhead = lax.broadcasted_iota(jnp.int32, (1, o.shape[1]), 1) // MB_HEAD_DIM
        o = jnp.where(rowhead == colhead, o, 0.0)
        o_ref[...] = jnp.sum(o.reshape(MB_HEADS, tdec, o.shape[1]), axis=0)


def _moba_dec(q, kn, vn, cache_k, cache_v, page_table, layer):
    ns, w = q.shape
    bs, n_pages = page_table.shape
    tdec = ns // bs
    page = cache_k.shape[3]
    ppb = MB_BLOCK // page
    nb = n_pages // ppb
    rows = MB_HEADS * tdec
    row_spec = pl.BlockSpec((tdec, w), lambda b, p, pt: (b, 0))
    kern = functools.partial(_moba_dec_kernel, ppb=ppb, tdec=tdec, nb=nb)
    return pl.pallas_call(
        kern,
        grid_spec=pltpu.PrefetchScalarGridSpec(
            num_scalar_prefetch=1,
            grid=(bs, nb),
            in_specs=[row_spec, row_spec, row_spec]
            + [_page_spec(layer, ppb, r, w, page) for r in range(ppb)] * 2,
            out_specs=row_spec,
            scratch_shapes=[pltpu.VMEM((rows, w), F32), pltpu.VMEM((nb, rows, LANES), F32),
                            pltpu.VMEM((nb, rows, LANES), F32), pltpu.VMEM((nb, rows, LANES), F32),
                            pltpu.VMEM((nb, rows, w), F32)],
        ),
        out_shape=jax.ShapeDtypeStruct((ns, w), F32),
        compiler_params=_cparams(("arbitrary", "arbitrary")),
        name="moba_dec",
    )(page_table, q, kn, vn, *([cache_k] * ppb), *([cache_v] * ppb))


def _mix_kernel(oa_ref, ob_ref, sg_ref, x_ref, gt_ref, sh2_ref, sc2_ref, g_ref, b_ref,
                wa_ref, wb_ref, wo_ref, x1_ref, h2_ref, h2b_ref, *, alpha):
    ya = _dot(oa_ref[...].astype(BF16), wa_ref[...])
    yb = _dot(ob_ref[...].astype(BF16), wb_ref[...])
    merged = sg_ref[:, :D_MODEL] * ya + sg_ref[:, D_MODEL:] * yb
    y = _dot(merged.astype(BF16), wo_ref[...])
    x1 = _layer_norm(alpha * x_ref[...] + gt_ref[...] * y, g_ref[...], b_ref[...])
    x1_ref[...] = x1
    h2 = x1 * (1.0 + sc2_ref[...]) + sh2_ref[...]
    h2_ref[...] = h2
    h2b_ref[...] = h2.astype(BF16)


def _mix_out(oa, ob, sg, x, mod1, mod2, per_token, tokens_per_seq, ln_g, ln_b, wa, wb, wo, alpha, tm):
    n, d = x.shape
    tiles_per_seq = max(tokens_per_seq // tm, 1)
    row = pl.BlockSpec((tm, d), lambda i: (i, 0))
    half = pl.BlockSpec((tm, QKV_W), lambda i: (i, 0))
    vec = pl.BlockSpec((1, d), lambda i: (0, 0))
    return pl.pallas_call(
        functools.partial(_mix_kernel, alpha=alpha),
        grid=(n // tm,),
        in_specs=[half, half, pl.BlockSpec((tm, 2 * d), lambda i: (i, 0)), row,
                  _mod_spec(per_token, tm, tiles_per_seq, 2),
                  _mod_spec(per_token, tm, tiles_per_seq, 0),
                  _mod_spec(per_token, tm, tiles_per_seq, 1),
                  vec, vec,
                  pl.BlockSpec((QKV_W, d), lambda i: (0, 0)),
                  pl.BlockSpec((QKV_W, d), lambda i: (0, 0)),
                  pl.BlockSpec((d, d), lambda i: (0, 0))],
        out_specs=[row, row, row],
        out_shape=[jax.ShapeDtypeStruct((n, d), F32), jax.ShapeDtypeStruct((n, d), F32),
                   jax.ShapeDtypeStruct((n, d), BF16)],
        compiler_params=_cparams(("arbitrary",)),
        name="mix_out",
    )(oa, ob, sg, x, mod1, mod2, mod2, ln_g.reshape(1, d), ln_b.reshape(1, d), wa, wb, wo)


PEER_NEXT = PEER_TOPK + 1
EXP_CLAMP = 80.0


def _top_values(x, count):
    vals = []
    for _ in range(count):
        m = jnp.max(x, axis=0)
        vals.append(m)
        x = jnp.where(x == m[None], -jnp.inf, x)
    return vals


def _route_kernel(h2_ref, wqh_ref, wql_ref, l1h_ref, l1l_ref, l2h_ref, l2l_ref, l2ph_ref, l2pl_ref,
                  thr_ref, e1_ref, e2_ref):
    hh, hl = _split(h2_ref[...])
    q = _dot3(hh, hl, wqh_ref[...], wql_ref[...])
    qh, ql = _split(q)
    tm = q.shape[0]
    nk, nh = PEER_NKEYS, PEER_HEADS
    s1 = _dot3_nt(l1h_ref[...], l1l_ref[...], qh, ql).reshape(nk, nh, tm)
    s2 = _dot3_nt(l2h_ref[...], l2l_ref[...], qh, ql).reshape(nk, nh, tm)
    s2p = _dot3_nt(l2ph_ref[...], l2pl_ref[...], qh, ql).reshape(nh, nk, tm)
    a = _top_values(s1, PEER_NEXT)
    b = _top_values(s2, PEER_NEXT)
    cands = [a[i] + b[j] for i in range(PEER_NEXT) for j in range(PEER_NEXT) if (i + 1) * (j + 1) <= PEER_NEXT]
    x = jnp.stack(cands, axis=0)
    t = _top_values(x, PEER_NEXT)
    tau = 0.5 * (t[PEER_TOPK - 1] + t[PEER_TOPK])
    m0 = a[0] + b[0]
    z = jnp.zeros_like(m0)
    for cnd in cands:
        z = z + jnp.where(cnd > tau, jnp.exp(cnd - m0), 0.0)
    thr_ref[...] = jnp.exp(jnp.minimum(tau[None] - s1 - b[0][None], EXP_CLAMP))
    e1_ref[...] = jnp.exp(s1 - a[0][None]) / z[None]
    e2_ref[...] = jnp.exp(s2p - jnp.max(s2p, axis=1, keepdims=True))


def _peer_route(h2, wqh, wql, lmats, tm):
    n, d = h2.shape
    nk, nh = PEER_NKEYS, PEER_HEADS
    wspec = pl.BlockSpec((d, d), lambda i: (0, 0))
    return pl.pallas_call(
        _route_kernel,
        grid=(n // tm,),
        in_specs=[pl.BlockSpec((tm, d), lambda i: (i, 0))] + [wspec] * 8,
        out_specs=[pl.BlockSpec((nk, nh, tm), lambda i: (0, 0, i)),
                   pl.BlockSpec((nk, nh, tm), lambda i: (0, 0, i)),
                   pl.BlockSpec((nh, nk, tm), lambda i: (0, 0, i))],
        out_shape=[jax.ShapeDtypeStruct((nk, nh, n), F32), jax.ShapeDtypeStruct((nk, nh, n), F32),
                   jax.ShapeDtypeStruct((nh, nk, n), F32)],
        compiler_params=_cparams(("arbitrary",)),
        name="peer_route",
    )(h2, wqh, wql, *lmats)


def _peer_lmats(keys):
    nh, _, nk, hd = keys.shape
    eye = jnp.eye(nh, dtype=F32)

    def blockdiag(c):
        sel = jnp.zeros((2,), F32).at[c].set(1.0)
        full = (keys[:, c][:, :, None, None, :] * eye[:, None, :, None, None]
                * sel[None, None, None, :, None])
        return full

    out = []
    for c, interleave in ((0, True), (1, True), (1, False)):
        full = blockdiag(c)
        if interleave:
            full = jnp.transpose(full, (1, 0, 2, 3, 4))
        m = full.reshape(nh * nk, nh * 2 * hd)
        out.extend(_split(m))
    return out


def _gelu(x):
    return 0.5 * x * (1.0 + lax.erf(x * (2.0 ** -0.5)))


def _peer_kernel(h2b_ref, u_ref, vt_ref, thr_ref, e1_ref, e2_ref, x1_ref, gt_ref, g_ref, b_ref,
                 o_ref, acc_sc, p_sc, *, gs, alpha):
    j = pl.program_id(1)
    nk, nh = PEER_NKEYS, PEER_HEADS
    tm = h2b_ref.shape[0]

    @pl.when(j == 0)
    def _init():
        acc_sc[...] = jnp.zeros(acc_sc.shape, F32)

    h2b = h2b_ref[...]
    tcw = min(LANES, tm)
    for gg in range(gs):
        g = j * gs + gg
        at = _dot_nt(u_ref[gg * nk:(gg + 1) * nk, :], h2b)
        thr8 = thr_ref[g]
        e18 = e1_ref[g]
        for tc in range(tm // tcw):
            sl = slice(tc * tcw, (tc + 1) * tcw)
            w = jnp.zeros((nk, tcw), F32)
            for h in range(nh):
                e2 = e2_ref[h, :, sl]
                w = w + jnp.where(e2 > thr8[h:h + 1, sl], e2, 0.0) * e18[h:h + 1, sl]
            p_sc[gg * nk:(gg + 1) * nk, sl] = (w * _gelu(at[:, sl])).astype(BF16)
    acc_sc[...] += _dot(vt_ref[...], p_sc[...])

    @pl.when(j == pl.num_programs(1) - 1)
    def _finish():
        f = acc_sc[...].T
        o_ref[...] = _layer_norm(alpha * x1_ref[...] + gt_ref[...] * f, g_ref[...], b_ref[...])


def _peer_dense(h2b, u_bf, vt_bf, thr, e1, e2, x1, mod2, per_token, tokens_per_seq, ln_g, ln_b, alpha, tm, gs):
    n, d = x1.shape
    nk, nh = PEER_NKEYS, PEER_HEADS
    tiles_per_seq = max(tokens_per_seq // tm, 1)
    if per_token:
        gt_spec = pl.BlockSpec((tm, d), lambda i, j: (i, 2))
    else:
        gt_spec = pl.BlockSpec((None, 1, d), lambda i, j: (i // tiles_per_seq, 0, 2))
    return pl.pallas_call(
        functools.partial(_peer_kernel, gs=gs, alpha=alpha),
        grid=(n // tm, nk // gs),
        in_specs=[pl.BlockSpec((tm, d), lambda i, j: (i, 0)),
                  pl.BlockSpec((gs * nk, d), lambda i, j: (j, 0)),
                  pl.BlockSpec((d, gs * nk), lambda i, j: (0, j)),
                  pl.BlockSpec((nk, nh, tm), lambda i, j: (0, 0, i)),
                  pl.BlockSpec((nk, nh, tm), lambda i, j: (0, 0, i)),
                  pl.BlockSpec((nh, nk, tm), lambda i, j: (0, 0, i)),
                  pl.BlockSpec((tm, d), lambda i, j: (i, 0)),
                  gt_spec,
                  pl.BlockSpec((1, d), lambda i, j: (0, 0)),
                  pl.BlockSpec((1, d), lambda i, j: (0, 0))],
        out_specs=pl.BlockSpec((tm, d), lambda i, j: (i, 0)),
        out_shape=jax.ShapeDtypeStruct((n, d), F32),
        scratch_shapes=[pltpu.VMEM((d, tm), F32), pltpu.VMEM((gs * nk, tm), BF16)],
        compiler_params=_cparams(("arbitrary", "arbitrary")),
        name="peer_dense",
    )(h2b, u_bf, vt_bf, thr, e1, e2, x1, mod2, ln_g.reshape(1, d), ln_b.reshape(1, d))


def _pick_tile(n, pref):
    t = pref
    while n % t:
        t //= 2
    return t


def kernel(x_prompt, x_sample, cache_dk, cache_dv, cache_mk, cache_mv, page_table, c_prompt, c_sample,
           ln_in_g, ln_in_b, w_ada_mix, b_ada_mix, w_in, diff_lambda, diff_subln_g, w_branch_a, w_branch_b,
           w_out, ln1_g, ln1_b, w_ada_ffn, b_ada_ffn, peer_wq, peer_keys, peer_u, peer_v, ln2_g, ln2_b):
    batch, seq, d = x_prompt.shape
    dec_batch, dec_seq, _ = x_sample.shape
    depth = w_in.shape[0]
    n_pool, _, page, _, _ = cache_dk.shape
    past_len = page_table.shape[1] * page
    alpha = (2 * depth) ** 0.25
    assert seq % MB_BLOCK == 0 and past_len % MB_BLOCK == 0 and MB_BLOCK % page == 0
    assert dec_seq <= MB_BLOCK and dec_seq % SUBLANES == 0

    w_in_bf = w_in.astype(BF16)
    wa_bf = w_branch_a.astype(BF16)
    wb_bf = w_branch_b.astype(BF16)
    wo_bf = w_out.astype(BF16)
    u_bf = peer_u.astype(BF16)
    vt_bf = jnp.swapaxes(peer_v, 1, 2).astype(BF16)
    wq_split = [_split(peer_wq[l]) for l in range(depth)]
    lmats = [_peer_lmats(peer_keys[l]) for l in range(depth)]
    wvt_bf = jnp.stack([jnp.swapaxes(w_in[:, :, 2 * QKV_W:3 * QKV_W], 1, 2),
                        jnp.swapaxes(w_in[:, :, 5 * QKV_W:6 * QKV_W], 1, 2)], axis=1).astype(BF16)
    cdk = cache_dk.reshape(n_pool, depth, page * DA_HEADS, 2 * DA_HEAD_DIM)
    cdv = cache_dv.reshape(n_pool, depth, page * DA_HEADS, 2 * DA_HEAD_DIM)
    cmk = jnp.transpose(cache_mk, (0, 1, 3, 4, 2)).reshape(n_pool, depth, QKV_W, page)
    cmv = jnp.transpose(cache_mv, (0, 1, 3, 4, 2)).reshape(n_pool, depth, QKV_W, page)
    g4 = jnp.tile(diff_subln_g, (1, DA_HEADS))

    r = batch + dec_batch
    r_pad = -(-r // SUBLANES) * SUBLANES
    c_all = jnp.concatenate([c_prompt, c_sample, jnp.zeros((r_pad - r, d), F32)], axis=0)
    mods_mix = _ada(c_all, w_ada_mix, b_ada_mix)
    mods_ffn = _ada(c_all, w_ada_ffn, b_ada_ffn)

    def run_group(x, row0, nseq, tlen, pos, is_prompt):
        n = nseq * tlen
        per_token = not is_prompt
        tm = _pick_tile(n, 512)
        if is_prompt:
            tm = min(tm, _pick_tile(tlen, 512))
            tabs = _rope_tables(pos)
        else:
            tabs = _rope_tables(jnp.tile(pos, tm // tlen))
        xs = _ln_rows(x.reshape(n, d), ln_in_g, ln_in_b, tm)
        rows = {k: [] for k in ("dk", "dv", "mk", "mv")}
        for l in range(depth):
            lam_init = 0.8 - 0.6 * math.exp(-0.3 * l)
            mod1 = _mod_operand(mods_mix[l, row0:row0 + nseq], per_token, tlen)
            mod2 = _mod_operand(mods_ffn[l, row0:row0 + nseq], per_token, tlen)
            (qa, ka, va, qb, kb, vb, qab, kab, vat, qbb, kbb, vbt, sg) = _inproj(
                xs, mod1, per_token, tlen, w_in_bf[l], wvt_bf[l], tabs, tm)
            if is_prompt:
                tq = _pick_tile(tlen, 512)
                oa = _attn_prompt("diff", qab, kab, vat,
                                  (diff_lambda[l], diff_subln_g[l].reshape(LANES, 1)),
                                  nseq, tlen, tq, tq, lam_init)
                km = _kmean(kb)
                ob = _attn_prompt("moba", qbb, kbb, vbt, (qb, km), nseq, tlen, tq, MB_BLOCK)
            else:
                oa = _diff_dec(qa, ka, va, diff_lambda[l], g4[l].reshape(1, QKV_W), cdk, cdv, page_table, l,
                               lam_init, 4 if page_table.shape[1] % 4 == 0 else 1)
                ob = _moba_dec(qb, kb, vb, cmk, cmv, page_table, l)
            x1, h2, h2b = _mix_out(oa, ob, sg, xs, mod1, mod2, per_token, tlen, ln1_g[l], ln1_b[l],
                                   wa_bf[l], wb_bf[l], wo_bf[l], alpha, tm)
            thr, e1, e2 = _peer_route(h2, wq_split[l][0], wq_split[l][1], lmats[l], _pick_tile(n, 256))
            xs = _peer_dense(h2b, u_bf[l], vt_bf[l], thr, e1, e2, x1, mod2, per_token, tlen,
                             ln2_g[l], ln2_b[l], alpha, tm, 4)
            rows["dk"].append(ka)
            rows["dv"].append(va)
            rows["mk"].append(kb)
            rows["mv"].append(vb)
        y = xs.reshape(nseq, tlen, d)
        dk = jnp.stack(rows["dk"], 0).reshape(depth, nseq, tlen, DA_HEADS, 2 * DA_HEAD_DIM).swapaxes(0, 1)
        dv = jnp.stack(rows["dv"], 0).reshape(depth, nseq, tlen, DA_HEADS, 2 * DA_HEAD_DIM).swapaxes(0, 1)
        mk = jnp.stack(rows["mk"], 0).reshape(depth, nseq, tlen, MB_HEADS, MB_HEAD_DIM).swapaxes(0, 1)
        mv = jnp.stack(rows["mv"], 0).reshape(depth, nseq, tlen, MB_HEADS, MB_HEAD_DIM).swapaxes(0, 1)
        return y, dk, dv, mk, mv

    yp, dkp, dvp, mkp, mvp = run_group(x_prompt, 0, batch, seq, jnp.arange(seq), True)
    ys, dks, dvs, mks, mvs = run_group(x_sample, batch, dec_batch, dec_seq, past_len + jnp.arange(dec_seq), False)
    return (yp, ys, dkp, dvp, mkp, mvp, dks, dvs, mks, mvs)
```

```python
import functools
import math

import numpy as np
import jax
import jax.numpy as jnp
from jax import lax
from jax.experimental import pallas as pl
from jax.experimental.pallas import tpu as pltpu

D_MODEL = 1024
DA_HEADS = 4
DA_HEAD_DIM = 64
MB_HEADS = 8
MB_HEAD_DIM = 64
MB_BLOCK = 256
MB_TOPK = 3
ROPE_THETA = 500000.0
ROPE_DIM = 16
PEER_HEADS = 8
PEER_NKEYS = 128
PEER_QDIM = 128
PEER_TOPK = 16
LN_EPS = 1e-5
QKV_W = 512
HEAD_CHUNK = 64

LANES = 128
SUBLANES = 8
VMEM_LIMIT_BYTES = 52 * 1024 * 1024

F32 = jnp.float32
BF16 = jnp.bfloat16
NEG_BIG = -1e30


def _cparams(sem, flags=None):
    return pltpu.CompilerParams(dimension_semantics=sem, vmem_limit_bytes=VMEM_LIMIT_BYTES, flags=flags)


def _dot(a, b):
    return jnp.dot(a, b, preferred_element_type=F32)


def _dot_nt(a, b):
    return lax.dot_general(a, b, (((1,), (1,)), ((), ())), preferred_element_type=F32)


def _split(x):
    hi = x.astype(BF16)
    lo = (x - hi.astype(F32)).astype(BF16)
    return hi, lo


def _dot3(ah, al, bh, bl):
    return _dot(ah, bh) + (_dot(ah, bl) + _dot(al, bh))


def _dot3_nt(ah, al, bh, bl):
    return _dot_nt(ah, bh) + (_dot_nt(ah, bl) + _dot_nt(al, bh))


def _layer_norm(x, g, b):
    mu = jnp.mean(x, axis=-1, keepdims=True)
    xc = x - mu
    var = jnp.mean(xc * xc, axis=-1, keepdims=True)
    return xc * lax.rsqrt(var + LN_EPS) * g + b


def _ada_kernel(c_ref, w_ref, b_ref, o_ref):
    c = c_ref[...]
    cs = c * (1.0 / (1.0 + jnp.exp(-c)))
    ch, cl = _split(cs)
    wh, wl = _split(w_ref[...])
    o_ref[...] = _dot3(ch, cl, wh, wl) + b_ref[...]


def _ada(c_all, w, b):
    depth, d, d3 = w.shape
    r = c_all.shape[0]
    tn = 1024
    return pl.pallas_call(
        _ada_kernel,
        grid=(depth, d3 // tn),
        in_specs=[
            pl.BlockSpec((r, d), lambda l, j: (0, 0)),
            pl.BlockSpec((None, d, tn), lambda l, j: (l, 0, j)),
            pl.BlockSpec((None, 1, tn), lambda l, j: (l, 0, j)),
        ],
        out_specs=pl.BlockSpec((None, r, tn), lambda l, j: (l, 0, j)),
        out_shape=jax.ShapeDtypeStruct((depth, r, d3), F32),
        compiler_params=_cparams(("arbitrary", "arbitrary")),
        name="ada",
    )(c_all, w, b.reshape(depth, 1, d3))


def _ln_kernel(x_ref, g_ref, b_ref, o_ref):
    o_ref[...] = _layer_norm(x_ref[...], g_ref[...], b_ref[...])


def _ln_rows(x, g, b, tm):
    n, d = x.shape
    return pl.pallas_call(
        _ln_kernel,
        grid=(n // tm,),
        in_specs=[
            pl.BlockSpec((tm, d), lambda i: (i, 0)),
            pl.BlockSpec((1, d), lambda i: (0, 0)),
            pl.BlockSpec((1, d), lambda i: (0, 0)),
        ],
        out_specs=pl.BlockSpec((tm, d), lambda i: (i, 0)),
        out_shape=jax.ShapeDtypeStruct((n, d), F32),
        compiler_params=_cparams(("arbitrary",)),
        name="ln_in",
    )(x, g.reshape(1, d), b.reshape(1, d))


def _mod_operand(mods, per_token, tokens_per_seq):
    if per_token:
        return jnp.repeat(mods, tokens_per_seq, axis=0)
    return mods.reshape(mods.shape[0], 1, mods.shape[1])


def _mod_spec(per_token, tm, tiles_per_seq, part):
    if per_token:
        return pl.BlockSpec((tm, D_MODEL), lambda i: (i, part))
    return pl.BlockSpec((None, 1, D_MODEL), lambda i: (i // tiles_per_seq, 0, part))


def _rope(z, c, s1, s2):
    reps = z.shape[1] // LANES
    c = jnp.tile(c, (1, reps))
    s1 = jnp.tile(s1, (1, reps))
    s2 = jnp.tile(s2, (1, reps))
    half = ROPE_DIM // 2
    return z * c + pltpu.roll(z, half, 1) * s1 + pltpu.roll(z, z.shape[1] - half, 1) * s2


def _inproj_kernel(x_ref, sh_ref, sc_ref, w_ref, wvt_ref, c_ref, s1_ref, s2_ref,
                   qa_ref, ka_ref, va_ref, qb_ref, kb_ref, vb_ref,
                   qab_ref, kab_ref, vat_ref, qbb_ref, kbb_ref, vbt_ref, sg_ref):
    h = x_ref[...] * (1.0 + sc_ref[...]) + sh_ref[...]
    hb = h.astype(BF16)
    c = c_ref[...]
    s1 = s1_ref[...]
    s2 = s2_ref[...]
    f32_outs = (qa_ref, ka_ref, va_ref, qb_ref, kb_ref, vb_ref)
    bf_outs = (qab_ref, kab_ref, None, qbb_ref, kbb_ref, None)
    rotate = (True, True, False, True, True, False)
    for j in range(6):
        z = _dot(hb, w_ref[:, j * QKV_W:(j + 1) * QKV_W])
        if rotate[j]:
            z = _rope(z, c, s1, s2)
        f32_outs[j][...] = z
        if bf_outs[j] is not None:
            bf_outs[j][...] = z.astype(BF16)
    vat_ref[...] = _dot_nt(wvt_ref[0], hb).astype(BF16)
    vbt_ref[...] = _dot_nt(wvt_ref[1], hb).astype(BF16)
    for j in range(4):
        z = _dot(hb, w_ref[:, (6 + j) * QKV_W:(7 + j) * QKV_W])
        sg_ref[:, j * QKV_W:(j + 1) * QKV_W] = 1.0 / (1.0 + jnp.exp(-z))


def _inproj(x, mod, per_token, tokens_per_seq, w_bf, wvt_bf, rope_tabs, tm):
    n, d = x.shape
    in_w = w_bf.shape[1]
    tiles_per_seq = max(tokens_per_seq // tm, 1)
    ctab, s1tab, s2tab = rope_tabs
    n_tab = ctab.shape[0] // tm
    slab = pl.BlockSpec((tm, QKV_W), lambda i: (i, 0))
    slab_t = pl.BlockSpec((QKV_W, tm), lambda i: (0, i))
    tab = pl.BlockSpec((tm, LANES), lambda i: (i % n_tab, 0))
    row_f32 = jax.ShapeDtypeStruct((n, QKV_W), F32)
    row_bf = jax.ShapeDtypeStruct((n, QKV_W), BF16)
    col_bf = jax.ShapeDtypeStruct((QKV_W, n), BF16)
    outs = pl.pallas_call(
        _inproj_kernel,
        grid=(n // tm,),
        in_specs=[
            pl.BlockSpec((tm, d), lambda i: (i, 0)),
            _mod_spec(per_token, tm, tiles_per_seq, 0),
            _mod_spec(per_token, tm, tiles_per_seq, 1),
            pl.BlockSpec((d, in_w), lambda i: (0, 0)),
            pl.BlockSpec((2, QKV_W, d), lambda i: (0, 0, 0)),
            tab, tab, tab,
        ],
        out_specs=[slab] * 6 + [slab, slab, slab_t, slab, slab, slab_t]
        + [pl.BlockSpec((tm, 2 * D_MODEL), lambda i: (i, 0))],
        out_shape=[row_f32] * 6 + [row_bf, row_bf, col_bf, row_bf, row_bf, col_bf]
        + [jax.ShapeDtypeStruct((n, 2 * D_MODEL), F32)],
        compiler_params=_cparams(("arbitrary",)),
        name="inproj",
    )(x, mod, mod, w_bf, wvt_bf, ctab, s1tab, s2tab)
    return outs


def _rope_tables(pos):
    half = ROPE_DIM // 2
    inv_freq = ROPE_THETA ** (-jnp.arange(half, dtype=F32) / half)
    ang = pos.astype(F32)[:, None] * inv_freq[None, :]
    cos = jnp.cos(ang)
    sin = jnp.sin(ang)
    t = pos.shape[0]
    ones = jnp.ones((t, HEAD_CHUNK - ROPE_DIM), F32)
    zeros = jnp.zeros((t, HEAD_CHUNK - ROPE_DIM), F32)
    zh = jnp.zeros((t, half), F32)
    c = jnp.concatenate([cos, cos, ones], axis=1)
    s1 = jnp.concatenate([zh, sin, zeros], axis=1)
    s2 = jnp.concatenate([-sin, zh, zeros], axis=1)
    reps = LANES // HEAD_CHUNK
    return tuple(jnp.tile(a, (1, reps)) for a in (c, s1, s2))


def _kmean_kernel(k_ref, o_ref):
    k = k_ref[...]
    nb = k.shape[0] // MB_BLOCK
    o_ref[...] = jnp.sum(k.reshape(nb, MB_BLOCK, k.shape[1]), axis=1) * (1.0 / MB_BLOCK)


def _kmean(kb):
    n, w = kb.shape
    rows = SUBLANES * MB_BLOCK
    return pl.pallas_call(
        _kmean_kernel,
        grid=(n // rows,),
        in_specs=[pl.BlockSpec((rows, w), lambda i: (i, 0))],
        out_specs=pl.BlockSpec((SUBLANES, w), lambda i: (i, 0)),
        out_shape=jax.ShapeDtypeStruct((n // MB_BLOCK, w), F32),
        compiler_params=_cparams(("arbitrary",)),
        name="moba_kmean",
    )(kb)


def _moba_select(gate, own, nblk):
    nidx = lax.broadcasted_iota(jnp.int32, (nblk, 1), 0)
    rank = jnp.zeros(gate.shape, jnp.int32)
    for n2 in range(nblk):
        row = gate[n2:n2 + 1, :]
        beats = (row > gate) | ((row == gate) & (n2 < nidx))
        rank = rank + jnp.where(beats & (n2 < own), 1, 0)
    selected = (nidx == own) | ((nidx < own) & (rank < MB_TOPK))
    return jnp.where(selected, 0.0, NEG_BIG)


def _attn_kernel(qt_ref, kt_ref, *refs, mode, tq, tk, nblk, lam_init):
    if mode == "diff":
        q_ref, k_ref, vt_ref, lam_ref, g_ref, o_ref, qc_sc, m_sc, l_sc, acc_sc = refs
    else:
        q_ref, k_ref, vt_ref, qf_ref, km_ref, o_ref, qc_sc, m_sc, l_sc, acc_sc, sel_sc = refs
    step = pl.program_id(2)
    qi = qt_ref[step]
    ki = kt_ref[step]
    lane = lax.broadcasted_iota(jnp.int32, (1, LANES), 1)
    lo = lane < HEAD_CHUNK
    qpos = qi * tq + lax.broadcasted_iota(jnp.int32, (1, tq), 1)
    kpos = ki * tk + lax.broadcasted_iota(jnp.int32, (tk, 1), 0)
    vrows = acc_sc.shape[1]

    @pl.when(ki == 0)
    def _init():
        q = q_ref[...]
        scale = DA_HEAD_DIM ** -0.5
        qc_sc[0] = jnp.where(lo, q, jnp.zeros_like(q)) * scale
        qc_sc[1] = jnp.where(lo, jnp.zeros_like(q), q) * scale
        m_sc[...] = jnp.full(m_sc.shape, NEG_BIG, F32)
        l_sc[...] = jnp.zeros(l_sc.shape, F32)
        acc_sc[...] = jnp.zeros(acc_sc.shape, F32)
        if mode == "moba":
            qh, ql = _split(qf_ref[...])
            km = km_ref[...]
            own = qpos // MB_BLOCK
            for c in range(2):
                kh, kl = _split(jnp.where(lo if c == 0 else ~lo, km, 0.0))
                gate = _dot3_nt(kh, kl, qh, ql)
                sel_sc[c] = _moba_select(gate, own, nblk)

    k = k_ref[...]
    vt = vt_ref[...]
    causal = kpos <= qpos
    for c in range(2):
        s = _dot_nt(k, qc_sc[c])
        if mode == "moba":
            s = s + sel_sc[c, pl.ds(ki * (tk // MB_BLOCK), 1), :]
        s = jnp.where(causal, s, NEG_BIG)
        m_prev = m_sc[c]
        m_new = jnp.maximum(m_prev, jnp.max(s, axis=0, keepdims=True))
        alpha = jnp.exp(m_prev - m_new)
        p = jnp.exp(s - m_new)
        l_sc[c] = alpha * l_sc[c] + jnp.sum(p, axis=0, keepdims=True)
        vc = vt if mode == "diff" else vt[c * vrows:(c + 1) * vrows, :]
        acc_sc[c] = alpha * acc_sc[c] + _dot(vc, p.astype(BF16))
        m_sc[c] = m_new

    last_ki = ((qi + 1) * tq - 1) // tk

    @pl.when(ki == last_ki)
    def _finish():
        o0 = acc_sc[0] / l_sc[0]
        o1 = acc_sc[1] / l_sc[1]
        if mode == "diff":
            lp = lam_ref[...]
            lam = (jnp.exp(jnp.sum(lp[0:1] * lp[1:2], axis=1, keepdims=True))
                   - jnp.exp(jnp.sum(lp[2:3] * lp[3:4], axis=1, keepdims=True)) + lam_init)
            o = o0 - lam * o1
            o = o * lax.rsqrt(jnp.mean(o * o, axis=0, keepdims=True) + LN_EPS) * g_ref[...] * (1.0 - lam_init)
        else:
            o = jnp.concatenate([o0, o1], axis=0)
        o_ref[...] = o.T.astype(o_ref.dtype)


def _tri_steps(t, tq, tk):
    qs, ks = [], []
    for qi in range(t // tq):
        for ki in range(((qi + 1) * tq - 1) // tk + 1):
            qs.append(qi)
            ks.append(ki)
    return np.asarray(qs, np.int32), np.asarray(ks, np.int32)


def _attn_prompt(mode, q, k, vt, extra, batch, t, tq, tk, lam_init=0.0):
    n, w = q.shape
    nslab = w // LANES
    nblk = t // MB_BLOCK
    qs, ks = _tri_steps(t, tq, tk)
    nq, nk = t // tq, t // tk
    qspec = pl.BlockSpec((tq, LANES), lambda b, h, s, qt, kt: (b * nq + qt[s], h))
    kspec = pl.BlockSpec((tk, LANES), lambda b, h, s, qt, kt: (b * nk + kt[s], h))
    vspec = pl.BlockSpec((LANES, tk), lambda b, h, s, qt, kt: (h, b * nk + kt[s]))
    if mode == "diff":
        extra_specs = [pl.BlockSpec((4, DA_HEAD_DIM), lambda b, h, s, qt, kt: (0, 0)),
                       pl.BlockSpec((LANES, 1), lambda b, h, s, qt, kt: (0, 0))]
        scratch = []
        vrows = LANES
    else:
        assert tk == MB_BLOCK
        extra_specs = [qspec, pl.BlockSpec((nblk, LANES), lambda b, h, s, qt, kt: (b, h))]
        scratch = [pltpu.VMEM((2, nblk, tq), F32)]
        vrows = HEAD_CHUNK
    kern = functools.partial(_attn_kernel, mode=mode, tq=tq, tk=tk, nblk=nblk, lam_init=lam_init)
    return pl.pallas_call(
        kern,
        grid_spec=pltpu.PrefetchScalarGridSpec(
            num_scalar_prefetch=2,
            grid=(batch, nslab, len(qs)),
            in_specs=[qspec, kspec, vspec] + extra_specs,
            out_specs=qspec,
            scratch_shapes=[pltpu.VMEM((2, tq, LANES), BF16),
                            pltpu.VMEM((2, 1, tq), F32), pltpu.VMEM((2, 1, tq), F32),
                            pltpu.VMEM((2, vrows, tq), F32)] + scratch,
        ),
        out_shape=jax.ShapeDtypeStruct((n, w), BF16),
        compiler_params=_cparams(("arbitrary", "arbitrary", "arbitrary")),
        name="attn_" + mode,
    )(jnp.asarray(qs), jnp.asarray(ks), q, k, vt, *extra)


def _stack_heads(q8, n_groups):
    t, w = q8.shape
    rows = n_groups * t
    qt = jnp.concatenate([q8] * n_groups, axis=0)
    rowgrp = lax.broadcasted_iota(jnp.int32, (rows, 1), 0) // t
    colgrp = lax.broadcasted_iota(jnp.int32, (1, w), 1) // (w // n_groups)
    diag = rowgrp == colgrp
    return jnp.where(diag, qt, 0.0), diag


def _softmax_update(s_list, v_list, m_prev, l_prev, acc_prev):
    m_cur = functools.reduce(jnp.maximum, [jnp.max(s, axis=1, keepdims=True) for s in s_list])
    m_new = jnp.maximum(m_prev, m_cur)
    alpha = jnp.exp(m_prev - m_new)
    l = alpha * l_prev
    acc = alpha * acc_prev
    for s, v in zip(s_list, v_list):
        p = jnp.exp(s - m_new)
        l = l + jnp.sum(p, axis=1, keepdims=True)
        acc = acc + _dot(p.astype(v.dtype), v)
    return m_new, l, acc


def _diff_dec_kernel(pt_ref, q_ref, kn_ref, vn_ref, lam_ref, g_ref, *refs, pp, tdec, lam_init):
    k_refs = refs[:pp]
    v_refs = refs[pp:2 * pp]
    o_ref = refs[2 * pp]
    nh = DA_HEADS
    rows = 2 * nh * tdec
    lane = lax.broadcasted_iota(jnp.int32, (1, LANES), 1)
    lo = lane < HEAD_CHUNK
    rowhead = lax.broadcasted_iota(jnp.int32, (rows, 1), 0) // (2 * tdec)

    q = q_ref[...] * (DA_HEAD_DIM ** -0.5)
    parts = []
    for h in range(nh):
        qh = q[:, h * LANES:(h + 1) * LANES]
        parts.append(jnp.where(lo, qh, 0.0))
        parts.append(jnp.where(lo, 0.0, qh))
    qs = jnp.concatenate(parts, axis=0)
    qb = qs.astype(BF16)
    page_rows = k_refs[0].shape[0]
    same = rowhead == lax.broadcasted_iota(jnp.int32, (1, page_rows), 1) % nh
    s_list = [jnp.where(same, _dot_nt(qb, k_refs[r][...].astype(BF16)), NEG_BIG) for r in range(pp)]
    v_list = [v_refs[r][...].astype(BF16) for r in range(pp)]

    kn = kn_ref[...]
    vn = vn_ref[...]
    knr = jnp.concatenate([kn[:, h * LANES:(h + 1) * LANES] for h in range(nh)], axis=0)
    vnr = jnp.concatenate([vn[:, h * LANES:(h + 1) * LANES] for h in range(nh)], axis=0)
    col = lax.broadcasted_iota(jnp.int32, (1, nh * tdec), 1)
    trow = lax.broadcasted_iota(jnp.int32, (rows, 1), 0) % tdec
    valid = (rowhead == col // tdec) & (col % tdec <= trow)
    s_list.append(jnp.where(valid, _dot_nt(qs, knr), NEG_BIG))
    v_list.append(vnr)

    _, l, acc = _softmax_update(s_list, v_list, jnp.full((rows, 1), NEG_BIG, F32),
                                jnp.zeros((rows, 1), F32), jnp.zeros((rows, LANES), F32))
    o = acc / l
    lp = lam_ref[...]
    lam = (jnp.exp(jnp.sum(lp[0:1] * lp[1:2], axis=1, keepdims=True))
           - jnp.exp(jnp.sum(lp[2:3] * lp[3:4], axis=1, keepdims=True)) + lam_init)
    outs = []
    for h in range(nh):
        oh = o[2 * h * tdec:(2 * h + 1) * tdec] - lam * o[(2 * h + 1) * tdec:(2 * h + 2) * tdec]
        outs.append(oh * lax.rsqrt(jnp.mean(oh * oh, axis=1, keepdims=True) + LN_EPS))
    o_ref[...] = jnp.concatenate(outs, axis=1) * g_ref[...] * (1.0 - lam_init)


def _page_spec(layer, idx, rows, w):
    return pl.BlockSpec((None, None, rows, w), lambda b, pt: (pt[b, idx], layer, 0, 0))


def _diff_dec(q, kn, vn, lam_p, g4, cache_k, cache_v, page_table, layer, lam_init):
    ns, w = q.shape
    bs, n_pages = page_table.shape
    tdec = ns // bs
    page_rows = cache_k.shape[2]
    row_spec = pl.BlockSpec((tdec, w), lambda b, pt: (b, 0))
    kern = functools.partial(_diff_dec_kernel, pp=n_pages, tdec=tdec, lam_init=lam_init)
    return pl.pallas_call(
        kern,
        grid_spec=pltpu.PrefetchScalarGridSpec(
            num_scalar_prefetch=1,
            grid=(bs,),
            in_specs=[row_spec, row_spec, row_spec,
                      pl.BlockSpec((4, DA_HEAD_DIM), lambda b, pt: (0, 0)),
                      pl.BlockSpec((1, w), lambda b, pt: (0, 0))]
            + [_page_spec(layer, r, page_rows, LANES) for r in range(n_pages)] * 2,
            out_specs=row_spec,
        ),
        out_shape=jax.ShapeDtypeStruct((ns, w), F32),
        compiler_params=_cparams(("arbitrary",)),
        name="diff_dec",
    )(page_table, q, kn, vn, lam_p, g4, *([cache_k] * n_pages), *([cache_v] * n_pages))


def _moba_dec_kernel(pt_ref, q_ref, kn_ref, vn_ref, *refs, ppb, tdec, nb):
    k_refs = refs[:nb * ppb]
    v_refs = refs[nb * ppb:2 * nb * ppb]
    o_ref = refs[2 * nb * ppb]
    scale = MB_HEAD_DIM ** -0.5
    rows = MB_HEADS * tdec

    qbd, _ = _stack_heads(q_ref[...], MB_HEADS)
    qb = (qbd * scale).astype(BF16)
    gates, ms, ls, accs = [], [], [], []
    for n in range(nb):
        s = jnp.concatenate([_dot(qb, k_refs[n * ppb + r][...].astype(BF16)) for r in range(ppb)], axis=1)
        page = s.shape[1] // ppb
        m_n = jnp.max(s, axis=1, keepdims=True)
        p = jnp.exp(s - m_n)
        acc = _dot_nt(p[:, :page].astype(BF16), v_refs[n * ppb][...].astype(BF16))
        for r in range(1, ppb):
            acc = acc + _dot_nt(p[:, r * page:(r + 1) * page].astype(BF16), v_refs[n * ppb + r][...].astype(BF16))
        gates.append(jnp.sum(s, axis=1, keepdims=True))
        ms.append(m_n)
        ls.append(jnp.sum(p, axis=1, keepdims=True))
        accs.append(acc)

    s_o = _dot_nt(qbd * scale, kn_ref[...])
    trow = lax.broadcasted_iota(jnp.int32, (rows, 1), 0) % tdec
    jcol = lax.broadcasted_iota(jnp.int32, (1, tdec), 1)
    s_o = jnp.where(jcol <= trow, s_o, NEG_BIG)
    m_o = jnp.max(s_o, axis=1, keepdims=True)
    p_o = jnp.exp(s_o - m_o)
    l_o = jnp.sum(p_o, axis=1, keepdims=True)
    a_o = _dot(p_o, vn_ref[...])
    sels = []
    for n in range(nb):
        rank = jnp.zeros((rows, 1), jnp.int32)
        for n2 in range(nb):
            if n2 == n:
                continue
            beats = (gates[n2] > gates[n]) | ((gates[n2] == gates[n]) & (n2 < n))
            rank = rank + jnp.where(beats, 1, 0)
        sels.append(rank < MB_TOPK)
    m = m_o
    for n in range(nb):
        m = jnp.maximum(m, jnp.where(sels[n], ms[n], NEG_BIG))
    w_o = jnp.exp(m_o - m)
    l = l_o * w_o
    acc = a_o * w_o
    for n in range(nb):
        w_n = jnp.where(sels[n], jnp.exp(ms[n] - m), 0.0)
        l = l + w_n * ls[n]
        acc = acc + w_n * accs[n]
    o = acc / l
    rowhead = lax.broadcasted_iota(jnp.int32, (rows, 1), 0) // tdec
    colhead = lax.broadcasted_iota(jnp.int32, (1, o.shape[1]), 1) // MB_HEAD_DIM
    o = jnp.where(rowhead == colhead, o, 0.0)
    o_ref[...] = jnp.sum(o.reshape(MB_HEADS, tdec, o.shape[1]), axis=0)


def _moba_dec(q, kn, vn, cache_k, cache_v, page_table, layer):
    ns, w = q.shape
    bs, n_pages = page_table.shape
    tdec = ns // bs
    page = cache_k.shape[3]
    ppb = MB_BLOCK // page
    nb = n_pages // ppb
    row_spec = pl.BlockSpec((tdec, w), lambda b, pt: (b, 0))
    kern = functools.partial(_moba_dec_kernel, ppb=ppb, tdec=tdec, nb=nb)
    return pl.pallas_call(
        kern,
        grid_spec=pltpu.PrefetchScalarGridSpec(
            num_scalar_prefetch=1,
            grid=(bs,),
            in_specs=[row_spec, row_spec, row_spec]
            + [_page_spec(layer, r, w, page) for r in range(n_pages)] * 2,
            out_specs=row_spec,
        ),
        out_shape=jax.ShapeDtypeStruct((ns, w), F32),
        compiler_params=_cparams(("arbitrary",)),
        name="moba_dec",
    )(page_table, q, kn, vn, *([cache_k] * n_pages), *([cache_v] * n_pages))


def _mix_kernel(oa_ref, ob_ref, sg_ref, x_ref, gt_ref, sh2_ref, sc2_ref, g_ref, b_ref,
                wa_ref, wb_ref, wo_ref, x1_ref, h2_ref, h2b_ref, *, alpha):
    ya = _dot(oa_ref[...].astype(BF16), wa_ref[...])
    yb = _dot(ob_ref[...].astype(BF16), wb_ref[...])
    merged = sg_ref[:, :D_MODEL] * ya + sg_ref[:, D_MODEL:] * yb
    y = _dot(merged.astype(BF16), wo_ref[...])
    x1 = _layer_norm(alpha * x_ref[...] + gt_ref[...] * y, g_ref[...], b_ref[...])
    x1_ref[...] = x1
    h2 = x1 * (1.0 + sc2_ref[...]) + sh2_ref[...]
    h2_ref[...] = h2
    h2b_ref[...] = h2.astype(BF16)


def _mix_out(oa, ob, sg, x, mod1, mod2, per_token, tokens_per_seq, ln_g, ln_b, wa, wb, wo, alpha, tm):
    n, d = x.shape
    tiles_per_seq = max(tokens_per_seq // tm, 1)
    row = pl.BlockSpec((tm, d), lambda i: (i, 0))
    half = pl.BlockSpec((tm, QKV_W), lambda i: (i, 0))
    vec = pl.BlockSpec((1, d), lambda i: (0, 0))
    return pl.pallas_call(
        functools.partial(_mix_kernel, alpha=alpha),
        grid=(n // tm,),
        in_specs=[half, half, pl.BlockSpec((tm, 2 * d), lambda i: (i, 0)), row,
                  _mod_spec(per_token, tm, tiles_per_seq, 2),
                  _mod_spec(per_token, tm, tiles_per_seq, 0),
                  _mod_spec(per_token, tm, tiles_per_seq, 1),
                  vec, vec,
                  pl.BlockSpec((QKV_W, d), lambda i: (0, 0)),
                  pl.BlockSpec((QKV_W, d), lambda i: (0, 0)),
                  pl.BlockSpec((d, d), lambda i: (0, 0))],
        out_specs=[row, row, row],
        out_shape=[jax.ShapeDtypeStruct((n, d), F32), jax.ShapeDtypeStruct((n, d), F32),
                   jax.ShapeDtypeStruct((n, d), BF16)],
        compiler_params=_cparams(("arbitrary",)),
        name="mix_out",
    )(oa, ob, sg, x, mod1, mod2, mod2, ln_g.reshape(1, d), ln_b.reshape(1, d), wa, wb, wo)


PEER_NEXT = PEER_TOPK + 1
PEER_GS = 4
EXP_CLAMP = 80.0


def _top_values(x, count):
    vals = []
    for _ in range(count):
        m = jnp.max(x, axis=0)
        vals.append(m)
        x = jnp.where(x == m[None], -jnp.inf, x)
    return vals


def _route_kernel(h2_ref, wqh_ref, wql_ref, l1h_ref, l1l_ref, l2h_ref, l2l_ref, l2ph_ref, l2pl_ref,
                  thr_ref, e1_ref, e2_ref):
    hh, hl = _split(h2_ref[...])
    q = _dot3(hh, hl, wqh_ref[...], wql_ref[...])
    qh, ql = _split(q)
    tm = q.shape[0]
    nk, nh = PEER_NKEYS, PEER_HEADS
    s1 = _dot3_nt(l1h_ref[...], l1l_ref[...], qh, ql).reshape(nk, nh, tm)
    s2 = _dot3_nt(l2h_ref[...], l2l_ref[...], qh, ql).reshape(nk, nh, tm)
    s2p = _dot3_nt(l2ph_ref[...], l2pl_ref[...], qh, ql).reshape(nh, nk, tm)
    a = _top_values(s1, PEER_NEXT)
    b = _top_values(s2, PEER_NEXT)
    cands = [a[i] + b[j] for i in range(PEER_NEXT) for j in range(PEER_NEXT) if (i + 1) * (j + 1) <= PEER_NEXT]
    x = jnp.stack(cands, axis=0)
    t = _top_values(x, PEER_NEXT)
    tau = 0.5 * (t[PEER_TOPK - 1] + t[PEER_TOPK])
    m0 = a[0] + b[0]
    z = jnp.zeros_like(m0)
    for cnd in cands:
        z = z + jnp.where(cnd > tau, jnp.exp(cnd - m0), 0.0)
    thr_ref[...] = jnp.exp(jnp.minimum(tau[None] - s1 - b[0][None], EXP_CLAMP))
    e1_ref[...] = jnp.exp(s1 - a[0][None]) / z[None]
    e2_ref[...] = jnp.exp(s2p - jnp.max(s2p, axis=1, keepdims=True))


def _peer_route(h2, wqh, wql, lmats, tm):
    n, d = h2.shape
    nk, nh = PEER_NKEYS, PEER_HEADS
    wspec = pl.BlockSpec((d, d), lambda i: (0, 0))
    return pl.pallas_call(
        _route_kernel,
        grid=(n // tm,),
        in_specs=[pl.BlockSpec((tm, d), lambda i: (i, 0))] + [wspec] * 8,
        out_specs=[pl.BlockSpec((nk, nh, tm), lambda i: (0, 0, i)),
                   pl.BlockSpec((nk, nh, tm), lambda i: (0, 0, i)),
                   pl.BlockSpec((nh, nk, tm), lambda i: (0, 0, i))],
        out_shape=[jax.ShapeDtypeStruct((nk, nh, n), F32), jax.ShapeDtypeStruct((nk, nh, n), F32),
                   jax.ShapeDtypeStruct((nh, nk, n), F32)],
        compiler_params=_cparams(("arbitrary",)),
        name="peer_route",
    )(h2, wqh, wql, *lmats)


def _peer_lmats(keys):
    nh, _, nk, hd = keys.shape
    eye = jnp.eye(nh, dtype=F32)

    def blockdiag(c):
        sel = jnp.zeros((2,), F32).at[c].set(1.0)
        full = (keys[:, c][:, :, None, None, :] * eye[:, None, :, None, None]
                * sel[None, None, None, :, None])
        return full

    out = []
    for c, interleave in ((0, True), (1, True), (1, False)):
        full = blockdiag(c)
        if interleave:
            full = jnp.transpose(full, (1, 0, 2, 3, 4))
        m = full.reshape(nh * nk, nh * 2 * hd)
        out.extend(_split(m))
    return out


def _gelu(x):
    return 0.5 * x * (1.0 + lax.erf(x * (2.0 ** -0.5)))


def _peer_kernel(h2b_ref, u_ref, vtp_ref, vta_ref, vtl_ref, thr_ref, e1_ref, e2_ref, x1_ref, gt_ref, g_ref, b_ref,
                 o_ref, acc_sc, pa_sc, pb_sc, *, gs, alpha):
    j = pl.program_id(1)
    nk, nh = PEER_NKEYS, PEER_HEADS
    tm = h2b_ref.shape[0]

    @pl.when(j == 0)
    def _init():
        acc_sc[...] = jnp.zeros(acc_sc.shape, F32)
        pb_sc[...] = jnp.zeros(pb_sc.shape, BF16)

    h2b = h2b_ref[...]
    tcw = min(LANES, tm)

    def build(half, p_sc):
        for gg in range(gs):
            g = (2 * j + half) * gs + gg
            r0 = (half * gs + gg) * nk
            at = _dot_nt(u_ref[r0:r0 + nk, :], h2b)
            thr8 = thr_ref[g]
            e18 = e1_ref[g]
            for tc in range(tm // tcw):
                sl = slice(tc * tcw, (tc + 1) * tcw)
                w = jnp.zeros((nk, tcw), F32)
                for h in range(nh):
                    e2 = e2_ref[h, :, sl]
                    w = w + jnp.where(e2 > thr8[h:h + 1, sl], e2, 0.0) * e18[h:h + 1, sl]
                p_sc[gg * nk:(gg + 1) * nk, sl] = (w * _gelu(at[:, sl])).astype(BF16)

    prev_b = _dot(vtp_ref[...], pb_sc[...])
    build(0, pa_sc)
    acc_sc[...] += prev_b + _dot(vta_ref[...], pa_sc[...])
    build(1, pb_sc)

    @pl.when(j == pl.num_programs(1) - 1)
    def _finish():
        f = (acc_sc[...] + _dot(vtl_ref[...], pb_sc[...])).T
        o_ref[...] = _layer_norm(alpha * x1_ref[...] + gt_ref[...] * f, g_ref[...], b_ref[...])


def _peer_dense(h2b, u_bf, vt_blk, thr, e1, e2, x1, mod2, per_token, tokens_per_seq, ln_g, ln_b, alpha, tm, gs):
    n, d = x1.shape
    nk, nh = PEER_NKEYS, PEER_HEADS
    nhalf = vt_blk.shape[0]
    hw = gs * nk
    tiles_per_seq = max(tokens_per_seq // tm, 1)
    if per_token:
        gt_spec = pl.BlockSpec((tm, d), lambda i, j: (i, 2))
    else:
        gt_spec = pl.BlockSpec((None, 1, d), lambda i, j: (i // tiles_per_seq, 0, 2))
    return pl.pallas_call(
        functools.partial(_peer_kernel, gs=gs, alpha=alpha),
        grid=(n // tm, nhalf // 2),
        in_specs=[pl.BlockSpec((tm, d), lambda i, j: (i, 0)),
                  pl.BlockSpec((2 * hw, d), lambda i, j: (j, 0)),
                  pl.BlockSpec((None, d, hw), lambda i, j: (jnp.maximum(2 * j - 1, 0), 0, 0)),
                  pl.BlockSpec((None, d, hw), lambda i, j: (2 * j, 0, 0)),
                  pl.BlockSpec((None, d, hw), lambda i, j: (nhalf - 1, 0, 0)),
                  pl.BlockSpec((nk, nh, tm), lambda i, j: (0, 0, i)),
                  pl.BlockSpec((nk, nh, tm), lambda i, j: (0, 0, i)),
                  pl.BlockSpec((nh, nk, tm), lambda i, j: (0, 0, i)),
                  pl.BlockSpec((tm, d), lambda i, j: (i, 0)),
                  gt_spec,
                  pl.BlockSpec((1, d), lambda i, j: (0, 0)),
                  pl.BlockSpec((1, d), lambda i, j: (0, 0))],
        out_specs=pl.BlockSpec((tm, d), lambda i, j: (i, 0)),
        out_shape=jax.ShapeDtypeStruct((n, d), F32),
        scratch_shapes=[pltpu.VMEM((d, tm), F32), pltpu.VMEM((hw, tm), BF16), pltpu.VMEM((hw, tm), BF16)],
        compiler_params=_cparams(("arbitrary", "arbitrary")),
        name="peer_dense",
    )(h2b, u_bf, vt_blk, vt_blk, vt_blk, thr, e1, e2, x1, mod2, ln_g.reshape(1, d), ln_b.reshape(1, d))


def _pick_tile(n, pref):
    t = pref
    while n % t:
        t //= 2
    return t


def kernel(x_prompt, x_sample, cache_dk, cache_dv, cache_mk, cache_mv, page_table, c_prompt, c_sample,
           ln_in_g, ln_in_b, w_ada_mix, b_ada_mix, w_in, diff_lambda, diff_subln_g, w_branch_a, w_branch_b,
           w_out, ln1_g, ln1_b, w_ada_ffn, b_ada_ffn, peer_wq, peer_keys, peer_u, peer_v, ln2_g, ln2_b):
    batch, seq, d = x_prompt.shape
    dec_batch, dec_seq, _ = x_sample.shape
    depth = w_in.shape[0]
    n_pool, _, page, _, _ = cache_dk.shape
    past_len = page_table.shape[1] * page
    alpha = (2 * depth) ** 0.25
    assert seq % MB_BLOCK == 0 and past_len % MB_BLOCK == 0 and MB_BLOCK % page == 0
    assert dec_seq <= MB_BLOCK and dec_seq % SUBLANES == 0

    w_in_bf = w_in.astype(BF16)
    wa_bf = w_branch_a.astype(BF16)
    wb_bf = w_branch_b.astype(BF16)
    wo_bf = w_out.astype(BF16)
    u_bf = peer_u.astype(BF16)
    n_exp = peer_v.shape[1]
    half_w = PEER_GS * PEER_NKEYS
    vt_blk = jnp.swapaxes(peer_v.reshape(depth, n_exp // half_w, half_w, d), 2, 3).astype(BF16)
    wq_split = [_split(peer_wq[l]) for l in range(depth)]
    lmats = [_peer_lmats(peer_keys[l]) for l in range(depth)]
    wvt_bf = jnp.stack([jnp.swapaxes(w_in[:, :, 2 * QKV_W:3 * QKV_W], 1, 2),
                        jnp.swapaxes(w_in[:, :, 5 * QKV_W:6 * QKV_W], 1, 2)], axis=1).astype(BF16)
    cdk = cache_dk.reshape(n_pool, depth, page * DA_HEADS, 2 * DA_HEAD_DIM)
    cdv = cache_dv.reshape(n_pool, depth, page * DA_HEADS, 2 * DA_HEAD_DIM)
    cmk = jnp.transpose(cache_mk, (0, 1, 3, 4, 2)).reshape(n_pool, depth, QKV_W, page)
    cmv = jnp.transpose(cache_mv, (0, 1, 3, 4, 2)).reshape(n_pool, depth, QKV_W, page)
    g4 = jnp.tile(diff_subln_g, (1, DA_HEADS))

    r = batch + dec_batch
    r_pad = -(-r // SUBLANES) * SUBLANES
    c_all = jnp.concatenate([c_prompt, c_sample, jnp.zeros((r_pad - r, d), F32)], axis=0)
    mods_mix = _ada(c_all, w_ada_mix, b_ada_mix)
    mods_ffn = _ada(c_all, w_ada_ffn, b_ada_ffn)

    def run_group(x, row0, nseq, tlen, pos, is_prompt):
        n = nseq * tlen
        per_token = not is_prompt
        tm = _pick_tile(n, 512)
        if is_prompt:
            tm = min(tm, _pick_tile(tlen, 512))
            tabs = _rope_tables(pos)
        else:
            tabs = _rope_tables(jnp.tile(pos, tm // tlen))
        xs = _ln_rows(x.reshape(n, d), ln_in_g, ln_in_b, tm)
        rows = {k: [] for k in ("dk", "dv", "mk", "mv")}
        for l in range(depth):
            lam_init = 0.8 - 0.6 * math.exp(-0.3 * l)
            mod1 = _mod_operand(mods_mix[l, row0:row0 + nseq], per_token, tlen)
            mod2 = _mod_operand(mods_ffn[l, row0:row0 + nseq], per_token, tlen)
            (qa, ka, va, qb, kb, vb, qab, kab, vat, qbb, kbb, vbt, sg) = _inproj(
                xs, mod1, per_token, tlen, w_in_bf[l], wvt_bf[l], tabs, tm)
            if is_prompt:
                tq = _pick_tile(tlen, 512)
                oa = _attn_prompt("diff", qab, kab, vat,
                                  (diff_lambda[l], diff_subln_g[l].reshape(LANES, 1)),
                                  nseq, tlen, tq, tq, lam_init)
                km = _kmean(kb)
                ob = _attn_prompt("moba", qbb, kbb, vbt, (qb, km), nseq, tlen, _pick_tile(tlen, 1024), MB_BLOCK)
            else:
                oa = _diff_dec(qa, ka, va, diff_lambda[l], g4[l].reshape(1, QKV_W), cdk, cdv, page_table, l,
                               lam_init)
                ob = _moba_dec(qb, kb, vb, cmk, cmv, page_table, l)
            x1, h2, h2b = _mix_out(oa, ob, sg, xs, mod1, mod2, per_token, tlen, ln1_g[l], ln1_b[l],
                                   wa_bf[l], wb_bf[l], wo_bf[l], alpha, tm)
            thr, e1, e2 = _peer_route(h2, wq_split[l][0], wq_split[l][1], lmats[l], _pick_tile(n, 256))
            xs = _peer_dense(h2b, u_bf[l], vt_blk[l], thr, e1, e2, x1, mod2, per_token, tlen,
                             ln2_g[l], ln2_b[l], alpha, tm, PEER_GS)
            rows["dk"].append(ka)
            rows["dv"].append(va)
            rows["mk"].append(kb)
            rows["mv"].append(vb)
        y = xs.reshape(nseq, tlen, d)
        dk = jnp.stack(rows["dk"], 0).reshape(depth, nseq, tlen, DA_HEADS, 2 * DA_HEAD_DIM).swapaxes(0, 1)
        dv = jnp.stack(rows["dv"], 0).reshape(depth, nseq, tlen, DA_HEADS, 2 * DA_HEAD_DIM).swapaxes(0, 1)
        mk = jnp.stack(rows["mk"], 0).reshape(depth, nseq, tlen, MB_HEADS, MB_HEAD_DIM).swapaxes(0, 1)
        mv = jnp.stack(rows["mv"], 0).reshape(depth, nseq, tlen, MB_HEADS, MB_HEAD_DIM).swapaxes(0, 1)
        return y, dk, dv, mk, mv

    yp, dkp, dvp, mkp, mvp = run_group(x_prompt, 0, batch, seq, jnp.arange(seq), True)
    ys, dks, dvs, mks, mvs = run_group(x_sample, batch, dec_batch, dec_seq, past_len + jnp.arange(dec_seq), False)
    return (yp, ys, dkp, dvp, mkp, mvp, dks, dvs, mks, mvs)
```

```python
import functools
import math

import numpy as np
import jax
import jax.numpy as jnp
from jax import lax
from jax.experimental import pallas as pl
from jax.experimental.pallas import tpu as pltpu

D_MODEL = 1024
DA_HEADS = 4
DA_HEAD_DIM = 64
MB_HEADS = 8
MB_HEAD_DIM = 64
MB_BLOCK = 256
MB_TOPK = 3
ROPE_THETA = 500000.0
ROPE_DIM = 16
PEER_HEADS = 8
PEER_NKEYS = 128
PEER_QDIM = 128
PEER_TOPK = 16
LN_EPS = 1e-5
QKV_W = 512
HEAD_CHUNK = 64

LANES = 128
SUBLANES = 8
VMEM_LIMIT_BYTES = 52 * 1024 * 1024

F32 = jnp.float32
BF16 = jnp.bfloat16
NEG_BIG = -1e30


def _cparams(sem, flags=None):
    return pltpu.CompilerParams(dimension_semantics=sem, vmem_limit_bytes=VMEM_LIMIT_BYTES, flags=flags)


def _dot(a, b):
    return jnp.dot(a, b, preferred_element_type=F32)


def _dot_nt(a, b):
    return lax.dot_general(a, b, (((1,), (1,)), ((), ())), preferred_element_type=F32)


def _split(x):
    hi = x.astype(BF16)
    lo = (x - hi.astype(F32)).astype(BF16)
    return hi, lo


def _dot3(ah, al, bh, bl):
    return _dot(ah, bh) + (_dot(ah, bl) + _dot(al, bh))


def _dot3_nt(ah, al, bh, bl):
    return _dot_nt(ah, bh) + (_dot_nt(ah, bl) + _dot_nt(al, bh))


def _layer_norm(x, g, b):
    mu = jnp.mean(x, axis=-1, keepdims=True)
    xc = x - mu
    var = jnp.mean(xc * xc, axis=-1, keepdims=True)
    return xc * lax.rsqrt(var + LN_EPS) * g + b


def _ada_kernel(c_ref, w_ref, b_ref, o_ref):
    c = c_ref[...]
    cs = c * (1.0 / (1.0 + jnp.exp(-c)))
    ch, cl = _split(cs)
    wh, wl = _split(w_ref[...])
    o_ref[...] = _dot3(ch, cl, wh, wl) + b_ref[...]


def _ada(c_all, w, b):
    depth, d, d3 = w.shape
    r = c_all.shape[0]
    tn = 1024
    return pl.pallas_call(
        _ada_kernel,
        grid=(depth, d3 // tn),
        in_specs=[
            pl.BlockSpec((r, d), lambda l, j: (0, 0)),
            pl.BlockSpec((None, d, tn), lambda l, j: (l, 0, j)),
            pl.BlockSpec((None, 1, tn), lambda l, j: (l, 0, j)),
        ],
        out_specs=pl.BlockSpec((None, r, tn), lambda l, j: (l, 0, j)),
        out_shape=jax.ShapeDtypeStruct((depth, r, d3), F32),
        compiler_params=_cparams(("arbitrary", "arbitrary")),
        name="ada",
    )(c_all, w, b.reshape(depth, 1, d3))


def _ln_kernel(x_ref, g_ref, b_ref, o_ref):
    o_ref[...] = _layer_norm(x_ref[...], g_ref[...], b_ref[...])


def _ln_rows(x, g, b, tm):
    n, d = x.shape
    return pl.pallas_call(
        _ln_kernel,
        grid=(n // tm,),
        in_specs=[
            pl.BlockSpec((tm, d), lambda i: (i, 0)),
            pl.BlockSpec((1, d), lambda i: (0, 0)),
            pl.BlockSpec((1, d), lambda i: (0, 0)),
        ],
        out_specs=pl.BlockSpec((tm, d), lambda i: (i, 0)),
        out_shape=jax.ShapeDtypeStruct((n, d), F32),
        compiler_params=_cparams(("arbitrary",)),
        name="ln_in",
    )(x, g.reshape(1, d), b.reshape(1, d))


def _mod_operand(mods, per_token, tokens_per_seq):
    if per_token:
        return jnp.repeat(mods, tokens_per_seq, axis=0)
    return mods.reshape(mods.shape[0], 1, mods.shape[1])


def _mod_spec(per_token, tm, tiles_per_seq, part):
    if per_token:
        return pl.BlockSpec((tm, D_MODEL), lambda i: (i, part))
    return pl.BlockSpec((None, 1, D_MODEL), lambda i: (i // tiles_per_seq, 0, part))


def _rope(z, c, s1, s2):
    reps = z.shape[1] // LANES
    c = jnp.tile(c, (1, reps))
    s1 = jnp.tile(s1, (1, reps))
    s2 = jnp.tile(s2, (1, reps))
    half = ROPE_DIM // 2
    return z * c + pltpu.roll(z, half, 1) * s1 + pltpu.roll(z, z.shape[1] - half, 1) * s2


def _inproj_kernel(x_ref, sh_ref, sc_ref, w_ref, wvt_ref, c_ref, s1_ref, s2_ref,
                   qa_ref, ka_ref, va_ref, qb_ref, kb_ref, vb_ref,
                   qab_ref, kab_ref, vat_ref, qbb_ref, kbb_ref, vbt_ref, sg_ref):
    h = x_ref[...] * (1.0 + sc_ref[...]) + sh_ref[...]
    hb = h.astype(BF16)
    c = c_ref[...]
    s1 = s1_ref[...]
    s2 = s2_ref[...]
    f32_outs = (qa_ref, ka_ref, va_ref, qb_ref, kb_ref, vb_ref)
    bf_outs = (qab_ref, kab_ref, None, qbb_ref, kbb_ref, None)
    rotate = (True, True, False, True, True, False)
    for j in range(6):
        z = _dot(hb, w_ref[:, j * QKV_W:(j + 1) * QKV_W])
        if rotate[j]:
            z = _rope(z, c, s1, s2)
        f32_outs[j][...] = z
        if bf_outs[j] is not None:
            bf_outs[j][...] = z.astype(BF16)
    vat_ref[...] = _dot_nt(wvt_ref[0], hb).astype(BF16)
    vbt_ref[...] = _dot_nt(wvt_ref[1], hb).astype(BF16)
    for j in range(4):
        z = _dot(hb, w_ref[:, (6 + j) * QKV_W:(7 + j) * QKV_W])
        sg_ref[:, j * QKV_W:(j + 1) * QKV_W] = 1.0 / (1.0 + jnp.exp(-z))


def _inproj(x, mod, per_token, tokens_per_seq, w_bf, wvt_bf, rope_tabs, tm):
    n, d = x.shape
    in_w = w_bf.shape[1]
    tiles_per_seq = max(tokens_per_seq // tm, 1)
    ctab, s1tab, s2tab = rope_tabs
    n_tab = ctab.shape[0] // tm
    slab = pl.BlockSpec((tm, QKV_W), lambda i: (i, 0))
    slab_t = pl.BlockSpec((QKV_W, tm), lambda i: (0, i))
    tab = pl.BlockSpec((tm, LANES), lambda i: (i % n_tab, 0))
    row_f32 = jax.ShapeDtypeStruct((n, QKV_W), F32)
    row_bf = jax.ShapeDtypeStruct((n, QKV_W), BF16)
    col_bf = jax.ShapeDtypeStruct((QKV_W, n), BF16)
    outs = pl.pallas_call(
        _inproj_kernel,
        grid=(n // tm,),
        in_specs=[
            pl.BlockSpec((tm, d), lambda i: (i, 0)),
            _mod_spec(per_token, tm, tiles_per_seq, 0),
            _mod_spec(per_token, tm, tiles_per_seq, 1),
            pl.BlockSpec((d, in_w), lambda i: (0, 0)),
            pl.BlockSpec((2, QKV_W, d), lambda i: (0, 0, 0)),
            tab, tab, tab,
        ],
        out_specs=[slab] * 6 + [slab, slab, slab_t, slab, slab, slab_t]
        + [pl.BlockSpec((tm, 2 * D_MODEL), lambda i: (i, 0))],
        out_shape=[row_f32] * 6 + [row_bf, row_bf, col_bf, row_bf, row_bf, col_bf]
        + [jax.ShapeDtypeStruct((n, 2 * D_MODEL), F32)],
        compiler_params=_cparams(("arbitrary",)),
        name="inproj",
    )(x, mod, mod, w_bf, wvt_bf, ctab, s1tab, s2tab)
    return outs


def _rope_tables(pos):
    half = ROPE_DIM // 2
    inv_freq = ROPE_THETA ** (-jnp.arange(half, dtype=F32) / half)
    ang = pos.astype(F32)[:, None] * inv_freq[None, :]
    cos = jnp.cos(ang)
    sin = jnp.sin(ang)
    t = pos.shape[0]
    ones = jnp.ones((t, HEAD_CHUNK - ROPE_DIM), F32)
    zeros = jnp.zeros((t, HEAD_CHUNK - ROPE_DIM), F32)
    zh = jnp.zeros((t, half), F32)
    c = jnp.concatenate([cos, cos, ones], axis=1)
    s1 = jnp.concatenate([zh, sin, zeros], axis=1)
    s2 = jnp.concatenate([-sin, zh, zeros], axis=1)
    reps = LANES // HEAD_CHUNK
    return tuple(jnp.tile(a, (1, reps)) for a in (c, s1, s2))


def _kmean_kernel(k_ref, o_ref):
    k = k_ref[...]
    nb = k.shape[0] // MB_BLOCK
    o_ref[...] = jnp.sum(k.reshape(nb, MB_BLOCK, k.shape[1]), axis=1) * (1.0 / MB_BLOCK)


def _kmean(kb):
    n, w = kb.shape
    rows = SUBLANES * MB_BLOCK
    return pl.pallas_call(
        _kmean_kernel,
        grid=(n // rows,),
        in_specs=[pl.BlockSpec((rows, w), lambda i: (i, 0))],
        out_specs=pl.BlockSpec((SUBLANES, w), lambda i: (i, 0)),
        out_shape=jax.ShapeDtypeStruct((n // MB_BLOCK, w), F32),
        compiler_params=_cparams(("arbitrary",)),
        name="moba_kmean",
    )(kb)


def _moba_select(gate, own, nblk):
    nidx = lax.broadcasted_iota(jnp.int32, (nblk, 1), 0)
    rank = jnp.zeros(gate.shape, jnp.int32)
    for n2 in range(nblk):
        row = gate[n2:n2 + 1, :]
        beats = (row > gate) | ((row == gate) & (n2 < nidx))
        rank = rank + jnp.where(beats & (n2 < own), 1, 0)
    selected = (nidx == own) | ((nidx < own) & (rank < MB_TOPK))
    return jnp.where(selected, 0.0, NEG_BIG)


def _attn_kernel(qt_ref, kt_ref, *refs, mode, tq, tk, nblk, lam_init):
    if mode == "diff":
        q_ref, k_ref, vt_ref, lam_ref, g_ref, o_ref, qc_sc, m_sc, l_sc, acc_sc = refs
    else:
        q_ref, k_ref, vt_ref, qf_ref, km_ref, o_ref, qc_sc, m_sc, l_sc, acc_sc, sel_sc = refs
    step = pl.program_id(2)
    qi = qt_ref[step]
    ki = kt_ref[step]
    lane = lax.broadcasted_iota(jnp.int32, (1, LANES), 1)
    lo = lane < HEAD_CHUNK
    qpos = qi * tq + lax.broadcasted_iota(jnp.int32, (1, tq), 1)
    kpos = ki * tk + lax.broadcasted_iota(jnp.int32, (tk, 1), 0)
    vrows = acc_sc.shape[1]

    nsl = q_ref.shape[1] // LANES

    @pl.when(ki == 0)
    def _init():
        scale = DA_HEAD_DIM ** -0.5
        m_sc[...] = jnp.full(m_sc.shape, NEG_BIG, F32)
        l_sc[...] = jnp.zeros(l_sc.shape, F32)
        acc_sc[...] = jnp.zeros(acc_sc.shape, F32)
        for sb in range(nsl):
            ls = slice(sb * LANES, (sb + 1) * LANES)
            q = q_ref[:, ls]
            qc_sc[2 * sb] = jnp.where(lo, q, jnp.zeros_like(q)) * scale
            qc_sc[2 * sb + 1] = jnp.where(lo, jnp.zeros_like(q), q) * scale
            if mode == "moba":
                qh, ql = _split(qf_ref[:, ls])
                km = km_ref[:, ls]
                own = qpos // MB_BLOCK
                for c in range(2):
                    kh, kl = _split(jnp.where(lo if c == 0 else ~lo, km, 0.0))
                    gate = _dot3_nt(kh, kl, qh, ql)
                    sel_sc[2 * sb + c] = _moba_select(gate, own, nblk)

    causal = kpos <= qpos
    for sb in range(nsl):
        ls = slice(sb * LANES, (sb + 1) * LANES)
        k = k_ref[:, ls]
        vt = vt_ref[ls, :]
        for c in range(2):
            ix = 2 * sb + c
            s = _dot_nt(k, qc_sc[ix])
            if mode == "moba":
                s = s + sel_sc[ix, pl.ds(ki * (tk // MB_BLOCK), 1), :]
            s = jnp.where(causal, s, NEG_BIG)
            m_prev = m_sc[ix]
            m_new = jnp.maximum(m_prev, jnp.max(s, axis=0, keepdims=True))
            alpha = jnp.exp(m_prev - m_new)
            p = jnp.exp(s - m_new)
            l_sc[ix] = alpha * l_sc[ix] + jnp.sum(p, axis=0, keepdims=True)
            vc = vt if mode == "diff" else vt[c * vrows:(c + 1) * vrows, :]
            acc_sc[ix] = alpha * acc_sc[ix] + _dot(vc, p.astype(BF16))
            m_sc[ix] = m_new

    last_ki = ((qi + 1) * tq - 1) // tk

    @pl.when(ki == last_ki)
    def _finish():
        for sb in range(nsl):
            o0 = acc_sc[2 * sb] / l_sc[2 * sb]
            o1 = acc_sc[2 * sb + 1] / l_sc[2 * sb + 1]
            if mode == "diff":
                lp = lam_ref[...]
                lam = (jnp.exp(jnp.sum(lp[0:1] * lp[1:2], axis=1, keepdims=True))
                       - jnp.exp(jnp.sum(lp[2:3] * lp[3:4], axis=1, keepdims=True)) + lam_init)
                o = o0 - lam * o1
                o = o * lax.rsqrt(jnp.mean(o * o, axis=0, keepdims=True) + LN_EPS) * g_ref[...] * (1.0 - lam_init)
            else:
                o = jnp.concatenate([o0, o1], axis=0)
            o_ref[:, sb * LANES:(sb + 1) * LANES] = o.T.astype(o_ref.dtype)


def _tri_steps(t, tq, tk):
    qs, ks = [], []
    for qi in range(t // tq):
        for ki in range(((qi + 1) * tq - 1) // tk + 1):
            qs.append(qi)
            ks.append(ki)
    return np.asarray(qs, np.int32), np.asarray(ks, np.int32)


def _attn_prompt(mode, q, k, vt, extra, batch, t, tq, tk, nsl, lam_init=0.0):
    n, w = q.shape
    nslab = w // LANES
    nblk = t // MB_BLOCK
    qs, ks = _tri_steps(t, tq, tk)
    nq, nk = t // tq, t // tk
    sw = nsl * LANES
    qspec = pl.BlockSpec((tq, sw), lambda b, h, s, qt, kt: (b * nq + qt[s], h))
    kspec = pl.BlockSpec((tk, sw), lambda b, h, s, qt, kt: (b * nk + kt[s], h))
    vspec = pl.BlockSpec((sw, tk), lambda b, h, s, qt, kt: (h, b * nk + kt[s]))
    if mode == "diff":
        extra_specs = [pl.BlockSpec((4, DA_HEAD_DIM), lambda b, h, s, qt, kt: (0, 0)),
                       pl.BlockSpec((LANES, 1), lambda b, h, s, qt, kt: (0, 0))]
        scratch = []
        vrows = LANES
    else:
        assert tk == MB_BLOCK
        extra_specs = [qspec, pl.BlockSpec((nblk, sw), lambda b, h, s, qt, kt: (b, h))]
        scratch = [pltpu.VMEM((2 * nsl, nblk, tq), F32)]
        vrows = HEAD_CHUNK
    kern = functools.partial(_attn_kernel, mode=mode, tq=tq, tk=tk, nblk=nblk, lam_init=lam_init)
    return pl.pallas_call(
        kern,
        grid_spec=pltpu.PrefetchScalarGridSpec(
            num_scalar_prefetch=2,
            grid=(batch, nslab // nsl, len(qs)),
            in_specs=[qspec, kspec, vspec] + extra_specs,
            out_specs=qspec,
            scratch_shapes=[pltpu.VMEM((2 * nsl, tq, LANES), BF16),
                            pltpu.VMEM((2 * nsl, 1, tq), F32), pltpu.VMEM((2 * nsl, 1, tq), F32),
                            pltpu.VMEM((2 * nsl, vrows, tq), F32)] + scratch,
        ),
        out_shape=jax.ShapeDtypeStruct((n, w), BF16),
        compiler_params=_cparams(("arbitrary", "arbitrary", "arbitrary")),
        name="attn_" + mode,
    )(jnp.asarray(qs), jnp.asarray(ks), q, k, vt, *extra)


def _stack_heads(q8, n_groups):
    t, w = q8.shape
    rows = n_groups * t
    qt = jnp.concatenate([q8] * n_groups, axis=0)
    rowgrp = lax.broadcasted_iota(jnp.int32, (rows, 1), 0) // t
    colgrp = lax.broadcasted_iota(jnp.int32, (1, w), 1) // (w // n_groups)
    diag = rowgrp == colgrp
    return jnp.where(diag, qt, 0.0), diag


def _softmax_update(s_list, v_list, m_prev, l_prev, acc_prev):
    m_cur = functools.reduce(jnp.maximum, [jnp.max(s, axis=1, keepdims=True) for s in s_list])
    m_new = jnp.maximum(m_prev, m_cur)
    alpha = jnp.exp(m_prev - m_new)
    l = alpha * l_prev
    acc = alpha * acc_prev
    for s, v in zip(s_list, v_list):
        p = jnp.exp(s - m_new)
        l = l + jnp.sum(p, axis=1, keepdims=True)
        acc = acc + _dot(p.astype(v.dtype), v)
    return m_new, l, acc


def _diff_dec_kernel(pt_ref, q_ref, kn_ref, vn_ref, lam_ref, g_ref, *refs, pp, tdec, lam_init):
    k_refs = refs[:pp]
    v_refs = refs[pp:2 * pp]
    o_ref = refs[2 * pp]
    nh = DA_HEADS
    rows = 2 * nh * tdec
    lane = lax.broadcasted_iota(jnp.int32, (1, LANES), 1)
    lo = lane < HEAD_CHUNK
    rowhead = lax.broadcasted_iota(jnp.int32, (rows, 1), 0) // (2 * tdec)

    q = q_ref[...] * (DA_HEAD_DIM ** -0.5)
    parts = []
    for h in range(nh):
        qh = q[:, h * LANES:(h + 1) * LANES]
        parts.append(jnp.where(lo, qh, 0.0))
        parts.append(jnp.where(lo, 0.0, qh))
    qs = jnp.concatenate(parts, axis=0)
    qb = qs.astype(BF16)
    page_rows = k_refs[0].shape[0]
    same = rowhead == lax.broadcasted_iota(jnp.int32, (1, page_rows), 1) % nh
    s_list = [jnp.where(same, _dot_nt(qb, k_refs[r][...].astype(BF16)), NEG_BIG) for r in range(pp)]
    v_list = [v_refs[r][...].astype(BF16) for r in range(pp)]

    kn = kn_ref[...]
    vn = vn_ref[...]
    knr = jnp.concatenate([kn[:, h * LANES:(h + 1) * LANES] for h in range(nh)], axis=0)
    vnr = jnp.concatenate([vn[:, h * LANES:(h + 1) * LANES] for h in range(nh)], axis=0)
    col = lax.broadcasted_iota(jnp.int32, (1, nh * tdec), 1)
    trow = lax.broadcasted_iota(jnp.int32, (rows, 1), 0) % tdec
    valid = (rowhead == col // tdec) & (col % tdec <= trow)
    s_list.append(jnp.where(valid, _dot_nt(qs, knr), NEG_BIG))
    v_list.append(vnr)

    _, l, acc = _softmax_update(s_list, v_list, jnp.full((rows, 1), NEG_BIG, F32),
                                jnp.zeros((rows, 1), F32), jnp.zeros((rows, LANES), F32))
    o = acc / l
    lp = lam_ref[...]
    lam = (jnp.exp(jnp.sum(lp[0:1] * lp[1:2], axis=1, keepdims=True))
           - jnp.exp(jnp.sum(lp[2:3] * lp[3:4], axis=1, keepdims=True)) + lam_init)
    outs = []
    for h in range(nh):
        oh = o[2 * h * tdec:(2 * h + 1) * tdec] - lam * o[(2 * h + 1) * tdec:(2 * h + 2) * tdec]
        outs.append(oh * lax.rsqrt(jnp.mean(oh * oh, axis=1, keepdims=True) + LN_EPS))
    o_ref[...] = jnp.concatenate(outs, axis=1) * g_ref[...] * (1.0 - lam_init)


def _page_spec(layer, idx, rows, w):
    return pl.BlockSpec((None, None, rows, w), lambda b, pt: (pt[b, idx], layer, 0, 0))


def _diff_dec(q, kn, vn, lam_p, g4, cache_k, cache_v, page_table, layer, lam_init):
    ns, w = q.shape
    bs, n_pages = page_table.shape
    tdec = ns // bs
    page_rows = cache_k.shape[2]
    row_spec = pl.BlockSpec((tdec, w), lambda b, pt: (b, 0))
    kern = functools.partial(_diff_dec_kernel, pp=n_pages, tdec=tdec, lam_init=lam_init)
    return pl.pallas_call(
        kern,
        grid_spec=pltpu.PrefetchScalarGridSpec(
            num_scalar_prefetch=1,
            grid=(bs,),
            in_specs=[row_spec, row_spec, row_spec,
                      pl.BlockSpec((4, DA_HEAD_DIM), lambda b, pt: (0, 0)),
                      pl.BlockSpec((1, w), lambda b, pt: (0, 0))]
            + [_page_spec(layer, r, page_rows, LANES) for r in range(n_pages)] * 2,
            out_specs=row_spec,
        ),
        out_shape=jax.ShapeDtypeStruct((ns, w), F32),
        compiler_params=_cparams(("arbitrary",)),
        name="diff_dec",
    )(page_table, q, kn, vn, lam_p, g4, *([cache_k] * n_pages), *([cache_v] * n_pages))


def _moba_dec_kernel(pt_ref, q_ref, kn_ref, vn_ref, *refs, ppb, tdec, nb):
    k_refs = refs[:nb * ppb]
    v_refs = refs[nb * ppb:2 * nb * ppb]
    o_ref = refs[2 * nb * ppb]
    scale = MB_HEAD_DIM ** -0.5
    rows = MB_HEADS * tdec

    qbd, _ = _stack_heads(q_ref[...], MB_HEADS)
    qb = (qbd * scale).astype(BF16)
    gates, ms, ls, accs = [], [], [], []
    for n in range(nb):
        s = jnp.concatenate([_dot(qb, k_refs[n * ppb + r][...].astype(BF16)) for r in range(ppb)], axis=1)
        page = s.shape[1] // ppb
        m_n = jnp.max(s, axis=1, keepdims=True)
        p = jnp.exp(s - m_n)
        acc = _dot_nt(p[:, :page].astype(BF16), v_refs[n * ppb][...].astype(BF16))
        for r in range(1, ppb):
            acc = acc + _dot_nt(p[:, r * page:(r + 1) * page].astype(BF16), v_refs[n * ppb + r][...].astype(BF16))
        gates.append(jnp.sum(s, axis=1, keepdims=True))
        ms.append(m_n)
        ls.append(jnp.sum(p, axis=1, keepdims=True))
        accs.append(acc)

    s_o = _dot_nt(qbd * scale, kn_ref[...])
    trow = lax.broadcasted_iota(jnp.int32, (rows, 1), 0) % tdec
    jcol = lax.broadcasted_iota(jnp.int32, (1, tdec), 1)
    s_o = jnp.where(jcol <= trow, s_o, NEG_BIG)
    m_o = jnp.max(s_o, axis=1, keepdims=True)
    p_o = jnp.exp(s_o - m_o)
    l_o = jnp.sum(p_o, axis=1, keepdims=True)
    a_o = _dot(p_o, vn_ref[...])
    sels = []
    for n in range(nb):
        rank = jnp.zeros((rows, 1), jnp.int32)
        for n2 in range(nb):
            if n2 == n:
                continue
            beats = (gates[n2] > gates[n]) | ((gates[n2] == gates[n]) & (n2 < n))
            rank = rank + jnp.where(beats, 1, 0)
        sels.append(rank < MB_TOPK)
    m = m_o
    for n in range(nb):
        m = jnp.maximum(m, jnp.where(sels[n], ms[n], NEG_BIG))
    w_o = jnp.exp(m_o - m)
    l = l_o * w_o
    acc = a_o * w_o
    for n in range(nb):
        w_n = jnp.where(sels[n], jnp.exp(ms[n] - m), 0.0)
        l = l + w_n * ls[n]
        acc = acc + w_n * accs[n]
    o = acc / l
    rowhead = lax.broadcasted_iota(jnp.int32, (rows, 1), 0) // tdec
    colhead = lax.broadcasted_iota(jnp.int32, (1, o.shape[1]), 1) // MB_HEAD_DIM
    o = jnp.where(rowhead == colhead, o, 0.0)
    o_ref[...] = jnp.sum(o.reshape(MB_HEADS, tdec, o.shape[1]), axis=0)


def _moba_dec(q, kn, vn, cache_k, cache_v, page_table, layer):
    ns, w = q.shape
    bs, n_pages = page_table.shape
    tdec = ns // bs
    page = cache_k.shape[3]
    ppb = MB_BLOCK // page
    nb = n_pages // ppb
    row_spec = pl.BlockSpec((tdec, w), lambda b, pt: (b, 0))
    kern = functools.partial(_moba_dec_kernel, ppb=ppb, tdec=tdec, nb=nb)
    return pl.pallas_call(
        kern,
        grid_spec=pltpu.PrefetchScalarGridSpec(
            num_scalar_prefetch=1,
            grid=(bs,),
            in_specs=[row_spec, row_spec, row_spec]
            + [_page_spec(layer, r, w, page) for r in range(n_pages)] * 2,
            out_specs=row_spec,
        ),
        out_shape=jax.ShapeDtypeStruct((ns, w), F32),
        compiler_params=_cparams(("arbitrary",)),
        name="moba_dec",
    )(page_table, q, kn, vn, *([cache_k] * n_pages), *([cache_v] * n_pages))


def _mix_kernel(oa_ref, ob_ref, sg_ref, x_ref, gt_ref, sh2_ref, sc2_ref, g_ref, b_ref,
                wa_ref, wb_ref, wo_ref, x1_ref, h2_ref, h2b_ref, *, alpha):
    ya = _dot(oa_ref[...].astype(BF16), wa_ref[...])
    yb = _dot(ob_ref[...].astype(BF16), wb_ref[...])
    merged = sg_ref[:, :D_MODEL] * ya + sg_ref[:, D_MODEL:] * yb
    y = _dot(merged.astype(BF16), wo_ref[...])
    x1 = _layer_norm(alpha * x_ref[...] + gt_ref[...] * y, g_ref[...], b_ref[...])
    x1_ref[...] = x1
    h2 = x1 * (1.0 + sc2_ref[...]) + sh2_ref[...]
    h2_ref[...] = h2
    h2b_ref[...] = h2.astype(BF16)


def _mix_out(oa, ob, sg, x, mod1, mod2, per_token, tokens_per_seq, ln_g, ln_b, wa, wb, wo, alpha, tm):
    n, d = x.shape
    tiles_per_seq = max(tokens_per_seq // tm, 1)
    row = pl.BlockSpec((tm, d), lambda i: (i, 0))
    half = pl.BlockSpec((tm, QKV_W), lambda i: (i, 0))
    vec = pl.BlockSpec((1, d), lambda i: (0, 0))
    return pl.pallas_call(
        functools.partial(_mix_kernel, alpha=alpha),
        grid=(n // tm,),
        in_specs=[half, half, pl.BlockSpec((tm, 2 * d), lambda i: (i, 0)), row,
                  _mod_spec(per_token, tm, tiles_per_seq, 2),
                  _mod_spec(per_token, tm, tiles_per_seq, 0),
                  _mod_spec(per_token, tm, tiles_per_seq, 1),
                  vec, vec,
                  pl.BlockSpec((QKV_W, d), lambda i: (0, 0)),
                  pl.BlockSpec((QKV_W, d), lambda i: (0, 0)),
                  pl.BlockSpec((d, d), lambda i: (0, 0))],
        out_specs=[row, row, row],
        out_shape=[jax.ShapeDtypeStruct((n, d), F32), jax.ShapeDtypeStruct((n, d), F32),
                   jax.ShapeDtypeStruct((n, d), BF16)],
        compiler_params=_cparams(("arbitrary",)),
        name="mix_out",
    )(oa, ob, sg, x, mod1, mod2, mod2, ln_g.reshape(1, d), ln_b.reshape(1, d), wa, wb, wo)


PEER_NEXT = PEER_TOPK + 1
PEER_GS = 4
EXP_CLAMP = 80.0


def _top_values(x, count):
    vals = []
    for _ in range(count):
        m = jnp.max(x, axis=0)
        vals.append(m)
        x = jnp.where(x == m[None], -jnp.inf, x)
    return vals


def _route_kernel(h2_ref, wqh_ref, wql_ref, l1h_ref, l1l_ref, l2h_ref, l2l_ref, l2ph_ref, l2pl_ref,
                  thr_ref, e1_ref, e2_ref):
    hh, hl = _split(h2_ref[...])
    q = _dot3(hh, hl, wqh_ref[...], wql_ref[...])
    qh, ql = _split(q)
    tm = q.shape[0]
    nk, nh = PEER_NKEYS, PEER_HEADS
    s1 = _dot3_nt(l1h_ref[...], l1l_ref[...], qh, ql).reshape(nk, nh, tm)
    s2 = _dot3_nt(l2h_ref[...], l2l_ref[...], qh, ql).reshape(nk, nh, tm)
    s2p = _dot3_nt(l2ph_ref[...], l2pl_ref[...], qh, ql).reshape(nh, nk, tm)
    a = _top_values(s1, PEER_NEXT)
    b = _top_values(s2, PEER_NEXT)
    cands = [a[i] + b[j] for i in range(PEER_NEXT) for j in range(PEER_NEXT) if (i + 1) * (j + 1) <= PEER_NEXT]
    x = jnp.stack(cands, axis=0)
    t = _top_values(x, PEER_NEXT)
    tau = 0.5 * (t[PEER_TOPK - 1] + t[PEER_TOPK])
    m0 = a[0] + b[0]
    z = jnp.zeros_like(m0)
    for cnd in cands:
        z = z + jnp.where(cnd > tau, jnp.exp(cnd - m0), 0.0)
    thr_ref[...] = jnp.exp(jnp.minimum(tau[None] - s1 - b[0][None], EXP_CLAMP))
    e1_ref[...] = jnp.exp(s1 - a[0][None]) / z[None]
    e2_ref[...] = jnp.exp(s2p - jnp.max(s2p, axis=1, keepdims=True))


def _peer_route(h2, wqh, wql, lmats, tm):
    n, d = h2.shape
    nk, nh = PEER_NKEYS, PEER_HEADS
    wspec = pl.BlockSpec((d, d), lambda i: (0, 0))
    return pl.pallas_call(
        _route_kernel,
        grid=(n // tm,),
        in_specs=[pl.BlockSpec((tm, d), lambda i: (i, 0))] + [wspec] * 8,
        out_specs=[pl.BlockSpec((nk, nh, tm), lambda i: (0, 0, i)),
                   pl.BlockSpec((nk, nh, tm), lambda i: (0, 0, i)),
                   pl.BlockSpec((nh, nk, tm), lambda i: (0, 0, i))],
        out_shape=[jax.ShapeDtypeStruct((nk, nh, n), F32), jax.ShapeDtypeStruct((nk, nh, n), F32),
                   jax.ShapeDtypeStruct((nh, nk, n), F32)],
        compiler_params=_cparams(("arbitrary",)),
        name="peer_route",
    )(h2, wqh, wql, *lmats)


def _peer_lmats(keys):
    nh, _, nk, hd = keys.shape
    eye = jnp.eye(nh, dtype=F32)

    def blockdiag(c):
        sel = jnp.zeros((2,), F32).at[c].set(1.0)
        full = (keys[:, c][:, :, None, None, :] * eye[:, None, :, None, None]
                * sel[None, None, None, :, None])
        return full

    out = []
    for c, interleave in ((0, True), (1, True), (1, False)):
        full = blockdiag(c)
        if interleave:
            full = jnp.transpose(full, (1, 0, 2, 3, 4))
        m = full.reshape(nh * nk, nh * 2 * hd)
        out.extend(_split(m))
    return out


def _gelu(x):
    return 0.5 * x * (1.0 + lax.erf(x * (2.0 ** -0.5)))


def _peer_kernel(h2b_ref, u_ref, vtp_ref, vta_ref, vtl_ref, thr_ref, e1_ref, e2_ref, x1_ref, gt_ref, g_ref, b_ref,
                 o_ref, acc_sc, pa_sc, pb_sc, *, gs, alpha):
    j = pl.program_id(1)
    nk, nh = PEER_NKEYS, PEER_HEADS
    tm = h2b_ref.shape[0]

    @pl.when(j == 0)
    def _init():
        acc_sc[...] = jnp.zeros(acc_sc.shape, F32)
        pb_sc[...] = jnp.zeros(pb_sc.shape, BF16)

    h2b = h2b_ref[...]
    tcw = min(LANES, tm)

    npiece = max(tm // (2 * LANES), 1)
    pw = tm // npiece

    def build(half, p_sc, vt_prev_ref, p_prev_sc):
        for gg in range(gs):
            if gg * npiece % gs == 0:
                cs = slice(gg * npiece // gs * pw, (gg * npiece // gs + 1) * pw)
                acc_sc[:, cs] += _dot(vt_prev_ref[...], p_prev_sc[:, cs])
            g = (2 * j + half) * gs + gg
            r0 = (half * gs + gg) * nk
            at = _dot_nt(u_ref[r0:r0 + nk, :], h2b)
            thr8 = thr_ref[g]
            e18 = e1_ref[g]
            for tc in range(tm // tcw):
                sl = slice(tc * tcw, (tc + 1) * tcw)
                w = jnp.zeros((nk, tcw), F32)
                for h in range(nh):
                    e2 = e2_ref[h, :, sl]
                    w = w + jnp.where(e2 > thr8[h:h + 1, sl], e2, 0.0) * e18[h:h + 1, sl]
                p_sc[gg * nk:(gg + 1) * nk, sl] = (w * _gelu(at[:, sl])).astype(BF16)

    build(0, pa_sc, vtp_ref, pb_sc)
    build(1, pb_sc, vta_ref, pa_sc)

    @pl.when(j == pl.num_programs(1) - 1)
    def _finish():
        f = (acc_sc[...] + _dot(vtl_ref[...], pb_sc[...])).T
        o_ref[...] = _layer_norm(alpha * x1_ref[...] + gt_ref[...] * f, g_ref[...], b_ref[...])


def _peer_dense(h2b, u_bf, vt_blk, thr, e1, e2, x1, mod2, per_token, tokens_per_seq, ln_g, ln_b, alpha, tm, gs):
    n, d = x1.shape
    nk, nh = PEER_NKEYS, PEER_HEADS
    nhalf = vt_blk.shape[0]
    hw = gs * nk
    tiles_per_seq = max(tokens_per_seq // tm, 1)
    if per_token:
        gt_spec = pl.BlockSpec((tm, d), lambda i, j: (i, 2))
    else:
        gt_spec = pl.BlockSpec((None, 1, d), lambda i, j: (i // tiles_per_seq, 0, 2))
    return pl.pallas_call(
        functools.partial(_peer_kernel, gs=gs, alpha=alpha),
        grid=(n // tm, nhalf // 2),
        in_specs=[pl.BlockSpec((tm, d), lambda i, j: (i, 0)),
                  pl.BlockSpec((2 * hw, d), lambda i, j: (j, 0)),
                  pl.BlockSpec((None, d, hw), lambda i, j: (jnp.maximum(2 * j - 1, 0), 0, 0)),
                  pl.BlockSpec((None, d, hw), lambda i, j: (2 * j, 0, 0)),
                  pl.BlockSpec((None, d, hw), lambda i, j: (nhalf - 1, 0, 0)),
                  pl.BlockSpec((nk, nh, tm), lambda i, j: (0, 0, i)),
                  pl.BlockSpec((nk, nh, tm), lambda i, j: (0, 0, i)),
                  pl.BlockSpec((nh, nk, tm), lambda i, j: (0, 0, i)),
                  pl.BlockSpec((tm, d), lambda i, j: (i, 0)),
                  gt_spec,
                  pl.BlockSpec((1, d), lambda i, j: (0, 0)),
                  pl.BlockSpec((1, d), lambda i, j: (0, 0))],
        out_specs=pl.BlockSpec((tm, d), lambda i, j: (i, 0)),
        out_shape=jax.ShapeDtypeStruct((n, d), F32),
        scratch_shapes=[pltpu.VMEM((d, tm), F32), pltpu.VMEM((hw, tm), BF16), pltpu.VMEM((hw, tm), BF16)],
        compiler_params=_cparams(("arbitrary", "arbitrary")),
        name="peer_dense",
    )(h2b, u_bf, vt_blk, vt_blk, vt_blk, thr, e1, e2, x1, mod2, ln_g.reshape(1, d), ln_b.reshape(1, d))


def _pick_tile(n, pref):
    t = pref
    while n % t:
        t //= 2
    return t


def kernel(x_prompt, x_sample, cache_dk, cache_dv, cache_mk, cache_mv, page_table, c_prompt, c_sample,
           ln_in_g, ln_in_b, w_ada_mix, b_ada_mix, w_in, diff_lambda, diff_subln_g, w_branch_a, w_branch_b,
           w_out, ln1_g, ln1_b, w_ada_ffn, b_ada_ffn, peer_wq, peer_keys, peer_u, peer_v, ln2_g, ln2_b):
    batch, seq, d = x_prompt.shape
    dec_batch, dec_seq, _ = x_sample.shape
    depth = w_in.shape[0]
    n_pool, _, page, _, _ = cache_dk.shape
    past_len = page_table.shape[1] * page
    alpha = (2 * depth) ** 0.25
    assert seq % MB_BLOCK == 0 and past_len % MB_BLOCK == 0 and MB_BLOCK % page == 0
    assert dec_seq <= MB_BLOCK and dec_seq % SUBLANES == 0

    w_in_bf = w_in.astype(BF16)
    wa_bf = w_branch_a.astype(BF16)
    wb_bf = w_branch_b.astype(BF16)
    wo_bf = w_out.astype(BF16)
    u_bf = peer_u.astype(BF16)
    n_exp = peer_v.shape[1]
    half_w = PEER_GS * PEER_NKEYS
    vt_blk = jnp.swapaxes(peer_v.reshape(depth, n_exp // half_w, half_w, d), 2, 3).astype(BF16)
    wq_split = [_split(peer_wq[l]) for l in range(depth)]
    lmats = [_peer_lmats(peer_keys[l]) for l in range(depth)]
    wvt_bf = jnp.stack([jnp.swapaxes(w_in[:, :, 2 * QKV_W:3 * QKV_W], 1, 2),
                        jnp.swapaxes(w_in[:, :, 5 * QKV_W:6 * QKV_W], 1, 2)], axis=1).astype(BF16)
    cdk = cache_dk.reshape(n_pool, depth, page * DA_HEADS, 2 * DA_HEAD_DIM)
    cdv = cache_dv.reshape(n_pool, depth, page * DA_HEADS, 2 * DA_HEAD_DIM)
    cmk = jnp.transpose(cache_mk, (0, 1, 3, 4, 2)).reshape(n_pool, depth, QKV_W, page)
    cmv = jnp.transpose(cache_mv, (0, 1, 3, 4, 2)).reshape(n_pool, depth, QKV_W, page)
    g4 = jnp.tile(diff_subln_g, (1, DA_HEADS))

    r = batch + dec_batch
    r_pad = -(-r // SUBLANES) * SUBLANES
    c_all = jnp.concatenate([c_prompt, c_sample, jnp.zeros((r_pad - r, d), F32)], axis=0)
    mods_mix = _ada(c_all, w_ada_mix, b_ada_mix)
    mods_ffn = _ada(c_all, w_ada_ffn, b_ada_ffn)

    def run_group(x, row0, nseq, tlen, pos, is_prompt):
        n = nseq * tlen
        per_token = not is_prompt
        tm = _pick_tile(n, 512)
        if is_prompt:
            tm = min(tm, _pick_tile(tlen, 512))
            tabs = _rope_tables(pos)
        else:
            tabs = _rope_tables(jnp.tile(pos, tm // tlen))
        xs = _ln_rows(x.reshape(n, d), ln_in_g, ln_in_b, tm)
        rows = {k: [] for k in ("dk", "dv", "mk", "mv")}
        for l in range(depth):
            lam_init = 0.8 - 0.6 * math.exp(-0.3 * l)
            mod1 = _mod_operand(mods_mix[l, row0:row0 + nseq], per_token, tlen)
            mod2 = _mod_operand(mods_ffn[l, row0:row0 + nseq], per_token, tlen)
            (qa, ka, va, qb, kb, vb, qab, kab, vat, qbb, kbb, vbt, sg) = _inproj(
                xs, mod1, per_token, tlen, w_in_bf[l], wvt_bf[l], tabs, tm)
            if is_prompt:
                tq = _pick_tile(tlen, 512)
                oa = _attn_prompt("diff", qab, kab, vat,
                                  (diff_lambda[l], diff_subln_g[l].reshape(LANES, 1)),
                                  nseq, tlen, tq, tq, 2, lam_init)
                km = _kmean(kb)
                ob = _attn_prompt("moba", qbb, kbb, vbt, (qb, km), nseq, tlen, _pick_tile(tlen, 1024), MB_BLOCK, 2)
            else:
                oa = _diff_dec(qa, ka, va, diff_lambda[l], g4[l].reshape(1, QKV_W), cdk, cdv, page_table, l,
                               lam_init)
                ob = _moba_dec(qb, kb, vb, cmk, cmv, page_table, l)
            x1, h2, h2b = _mix_out(oa, ob, sg, xs, mod1, mod2, per_token, tlen, ln1_g[l], ln1_b[l],
                                   wa_bf[l], wb_bf[l], wo_bf[l], alpha, tm)
            thr, e1, e2 = _peer_route(h2, wq_split[l][0], wq_split[l][1], lmats[l], _pick_tile(n, 256))
            xs = _peer_dense(h2b, u_bf[l], vt_blk[l], thr, e1, e2, x1, mod2, per_token, tlen,
                             ln2_g[l], ln2_b[l], alpha, tm, PEER_GS)
            rows["dk"].append(ka)
            rows["dv"].append(va)
            rows["mk"].append(kb)
            rows["mv"].append(vb)
        y = xs.reshape(nseq, tlen, d)
        dk = jnp.stack(rows["dk"], 0).reshape(depth, nseq, tlen, DA_HEADS, 2 * DA_HEAD_DIM).swapaxes(0, 1)
        dv = jnp.stack(rows["dv"], 0).reshape(depth, nseq, tlen, DA_HEADS, 2 * DA_HEAD_DIM).swapaxes(0, 1)
        mk = jnp.stack(rows["mk"], 0).reshape(depth, nseq, tlen, MB_HEADS, MB_HEAD_DIM).swapaxes(0, 1)
        mv = jnp.stack(rows["mv"], 0).reshape(depth, nseq, tlen, MB_HEADS, MB_HEAD_DIM).swapaxes(0, 1)
        return y, dk, dv, mk, mv

    yp, dkp, dvp, mkp, mvp = run_group(x_prompt, 0, batch, seq, jnp.arange(seq), True)
    ys, dks, dvs, mks, mvs = run_group(x_sample, batch, dec_batch, dec_seq, past_len + jnp.arange(dec_seq), False)
    return (yp, ys, dkp, dvp, mkp, mvp, dks, dvs, mks, mvs)
```

```python
import functools
import math

import numpy as np
import jax
import jax.numpy as jnp
from jax import lax
from jax.experimental import pallas as pl
from jax.experimental.pallas import tpu as pltpu

D_MODEL = 1024
DA_HEADS = 4
DA_HEAD_DIM = 64
MB_HEADS = 8
MB_HEAD_DIM = 64
MB_BLOCK = 256
MB_TOPK = 3
ROPE_THETA = 500000.0
ROPE_DIM = 16
PEER_HEADS = 8
PEER_NKEYS = 128
PEER_QDIM = 128
PEER_TOPK = 16
LN_EPS = 1e-5
QKV_W = 512
HEAD_CHUNK = 64

LANES = 128
SUBLANES = 8
VMEM_LIMIT_BYTES = 52 * 1024 * 1024

F32 = jnp.float32
BF16 = jnp.bfloat16
NEG_BIG = -1e30


def _cparams(sem, flags=None):
    return pltpu.CompilerParams(dimension_semantics=sem, vmem_limit_bytes=VMEM_LIMIT_BYTES, flags=flags)


def _dot(a, b):
    return jnp.dot(a, b, preferred_element_type=F32)


def _dot_nt(a, b):
    return lax.dot_general(a, b, (((1,), (1,)), ((), ())), preferred_element_type=F32)


def _split(x):
    hi = x.astype(BF16)
    lo = (x - hi.astype(F32)).astype(BF16)
    return hi, lo


def _dot3(ah, al, bh, bl):
    return _dot(ah, bh) + (_dot(ah, bl) + _dot(al, bh))


def _dot3_nt(ah, al, bh, bl):
    return _dot_nt(ah, bh) + (_dot_nt(ah, bl) + _dot_nt(al, bh))


def _layer_norm(x, g, b):
    mu = jnp.mean(x, axis=-1, keepdims=True)
    xc = x - mu
    var = jnp.mean(xc * xc, axis=-1, keepdims=True)
    return xc * lax.rsqrt(var + LN_EPS) * g + b


def _ada_kernel(c_ref, w_ref, b_ref, o_ref):
    c = c_ref[...]
    cs = c * (1.0 / (1.0 + jnp.exp(-c)))
    ch, cl = _split(cs)
    wh, wl = _split(w_ref[...])
    o_ref[...] = _dot3(ch, cl, wh, wl) + b_ref[...]


def _ada(c_all, w, b):
    depth, d, d3 = w.shape
    r = c_all.shape[0]
    tn = 1024
    return pl.pallas_call(
        _ada_kernel,
        grid=(depth, d3 // tn),
        in_specs=[
            pl.BlockSpec((r, d), lambda l, j: (0, 0)),
            pl.BlockSpec((None, d, tn), lambda l, j: (l, 0, j)),
            pl.BlockSpec((None, 1, tn), lambda l, j: (l, 0, j)),
        ],
        out_specs=pl.BlockSpec((None, r, tn), lambda l, j: (l, 0, j)),
        out_shape=jax.ShapeDtypeStruct((depth, r, d3), F32),
        compiler_params=_cparams(("arbitrary", "arbitrary")),
        name="ada",
    )(c_all, w, b.reshape(depth, 1, d3))


def _ln_kernel(x_ref, g_ref, b_ref, o_ref):
    o_ref[...] = _layer_norm(x_ref[...], g_ref[...], b_ref[...])


def _ln_rows(x, g, b, tm):
    n, d = x.shape
    return pl.pallas_call(
        _ln_kernel,
        grid=(n // tm,),
        in_specs=[
            pl.BlockSpec((tm, d), lambda i: (i, 0)),
            pl.BlockSpec((1, d), lambda i: (0, 0)),
            pl.BlockSpec((1, d), lambda i: (0, 0)),
        ],
        out_specs=pl.BlockSpec((tm, d), lambda i: (i, 0)),
        out_shape=jax.ShapeDtypeStruct((n, d), F32),
        compiler_params=_cparams(("arbitrary",)),
        name="ln_in",
    )(x, g.reshape(1, d), b.reshape(1, d))


def _mod_operand(mods, per_token, tokens_per_seq):
    if per_token:
        return jnp.repeat(mods, tokens_per_seq, axis=0)
    return mods.reshape(mods.shape[0], 1, mods.shape[1])


def _mod_spec(per_token, tm, tiles_per_seq, part):
    if per_token:
        return pl.BlockSpec((tm, D_MODEL), lambda i: (i, part))
    return pl.BlockSpec((None, 1, D_MODEL), lambda i: (i // tiles_per_seq, 0, part))


def _rope(z, c, s1, s2):
    reps = z.shape[1] // LANES
    c = jnp.tile(c, (1, reps))
    s1 = jnp.tile(s1, (1, reps))
    s2 = jnp.tile(s2, (1, reps))
    half = ROPE_DIM // 2
    return z * c + pltpu.roll(z, half, 1) * s1 + pltpu.roll(z, z.shape[1] - half, 1) * s2


def _inproj_kernel(x_ref, sh_ref, sc_ref, w_ref, wvt_ref, c_ref, s1_ref, s2_ref,
                   qa_ref, ka_ref, va_ref, qb_ref, kb_ref, vb_ref,
                   qab_ref, kab_ref, vat_ref, qbb_ref, kbb_ref, vbt_ref, sg_ref):
    h = x_ref[...] * (1.0 + sc_ref[...]) + sh_ref[...]
    hb = h.astype(BF16)
    c = c_ref[...]
    s1 = s1_ref[...]
    s2 = s2_ref[...]
    f32_outs = (qa_ref, ka_ref, va_ref, qb_ref, kb_ref, vb_ref)
    bf_outs = (qab_ref, kab_ref, None, qbb_ref, kbb_ref, None)
    rotate = (True, True, False, True, True, False)
    for j in range(6):
        z = _dot(hb, w_ref[:, j * QKV_W:(j + 1) * QKV_W])
        if rotate[j]:
            z = _rope(z, c, s1, s2)
        f32_outs[j][...] = z
        if bf_outs[j] is not None:
            bf_outs[j][...] = z.astype(BF16)
    vat_ref[...] = _dot_nt(wvt_ref[0], hb).astype(BF16)
    vbt_ref[...] = _dot_nt(wvt_ref[1], hb).astype(BF16)
    for j in range(4):
        z = _dot(hb, w_ref[:, (6 + j) * QKV_W:(7 + j) * QKV_W])
        sg_ref[:, j * QKV_W:(j + 1) * QKV_W] = 1.0 / (1.0 + jnp.exp(-z))


def _inproj(x, mod, per_token, tokens_per_seq, w_bf, wvt_bf, rope_tabs, tm):
    n, d = x.shape
    in_w = w_bf.shape[1]
    tiles_per_seq = max(tokens_per_seq // tm, 1)
    ctab, s1tab, s2tab = rope_tabs
    n_tab = ctab.shape[0] // tm
    slab = pl.BlockSpec((tm, QKV_W), lambda i: (i, 0))
    slab_t = pl.BlockSpec((QKV_W, tm), lambda i: (0, i))
    tab = pl.BlockSpec((tm, LANES), lambda i: (i % n_tab, 0))
    row_f32 = jax.ShapeDtypeStruct((n, QKV_W), F32)
    row_bf = jax.ShapeDtypeStruct((n, QKV_W), BF16)
    col_bf = jax.ShapeDtypeStruct((QKV_W, n), BF16)
    outs = pl.pallas_call(
        _inproj_kernel,
        grid=(n // tm,),
        in_specs=[
            pl.BlockSpec((tm, d), lambda i: (i, 0)),
            _mod_spec(per_token, tm, tiles_per_seq, 0),
            _mod_spec(per_token, tm, tiles_per_seq, 1),
            pl.BlockSpec((d, in_w), lambda i: (0, 0)),
            pl.BlockSpec((2, QKV_W, d), lambda i: (0, 0, 0)),
            tab, tab, tab,
        ],
        out_specs=[slab] * 6 + [slab, slab, slab_t, slab, slab, slab_t]
        + [pl.BlockSpec((tm, 2 * D_MODEL), lambda i: (i, 0))],
        out_shape=[row_f32] * 6 + [row_bf, row_bf, col_bf, row_bf, row_bf, col_bf]
        + [jax.ShapeDtypeStruct((n, 2 * D_MODEL), F32)],
        compiler_params=_cparams(("arbitrary",)),
        name="inproj",
    )(x, mod, mod, w_bf, wvt_bf, ctab, s1tab, s2tab)
    return outs


def _rope_tables(pos):
    half = ROPE_DIM // 2
    inv_freq = ROPE_THETA ** (-jnp.arange(half, dtype=F32) / half)
    ang = pos.astype(F32)[:, None] * inv_freq[None, :]
    cos = jnp.cos(ang)
    sin = jnp.sin(ang)
    t = pos.shape[0]
    ones = jnp.ones((t, HEAD_CHUNK - ROPE_DIM), F32)
    zeros = jnp.zeros((t, HEAD_CHUNK - ROPE_DIM), F32)
    zh = jnp.zeros((t, half), F32)
    c = jnp.concatenate([cos, cos, ones], axis=1)
    s1 = jnp.concatenate([zh, sin, zeros], axis=1)
    s2 = jnp.concatenate([-sin, zh, zeros], axis=1)
    reps = LANES // HEAD_CHUNK
    return tuple(jnp.tile(a, (1, reps)) for a in (c, s1, s2))


def _kmean_kernel(k_ref, o_ref):
    k = k_ref[...]
    nb = k.shape[0] // MB_BLOCK
    o_ref[...] = jnp.sum(k.reshape(nb, MB_BLOCK, k.shape[1]), axis=1) * (1.0 / MB_BLOCK)


def _kmean(kb):
    n, w = kb.shape
    rows = SUBLANES * MB_BLOCK
    return pl.pallas_call(
        _kmean_kernel,
        grid=(n // rows,),
        in_specs=[pl.BlockSpec((rows, w), lambda i: (i, 0))],
        out_specs=pl.BlockSpec((SUBLANES, w), lambda i: (i, 0)),
        out_shape=jax.ShapeDtypeStruct((n // MB_BLOCK, w), F32),
        compiler_params=_cparams(("arbitrary",)),
        name="moba_kmean",
    )(kb)


def _moba_select(gate, own, nblk):
    nidx = lax.broadcasted_iota(jnp.int32, (nblk, 1), 0)
    rank = jnp.zeros(gate.shape, jnp.int32)
    for n2 in range(nblk):
        row = gate[n2:n2 + 1, :]
        beats = (row > gate) | ((row == gate) & (n2 < nidx))
        rank = rank + jnp.where(beats & (n2 < own), 1, 0)
    selected = (nidx == own) | ((nidx < own) & (rank < MB_TOPK))
    return jnp.where(selected, 0.0, NEG_BIG)


def _attn_kernel(qt_ref, kt_ref, *refs, mode, tq, tk, nblk, lam_init):
    if mode == "diff":
        q_ref, k_ref, vt_ref, lam_ref, g_ref, o_ref, qc_sc, m_sc, l_sc, acc_sc = refs
    else:
        q_ref, k_ref, vt_ref, qf_ref, km_ref, o_ref, qc_sc, m_sc, l_sc, acc_sc, sel_sc = refs
    step = pl.program_id(2)
    qi = qt_ref[step]
    ki = kt_ref[step]
    lane = lax.broadcasted_iota(jnp.int32, (1, LANES), 1)
    lo = lane < HEAD_CHUNK
    qpos = qi * tq + lax.broadcasted_iota(jnp.int32, (1, tq), 1)
    kpos = ki * tk + lax.broadcasted_iota(jnp.int32, (tk, 1), 0)
    vrows = acc_sc.shape[1]

    nsl = q_ref.shape[1] // LANES

    @pl.when(ki == 0)
    def _init():
        scale = DA_HEAD_DIM ** -0.5
        m_sc[...] = jnp.full(m_sc.shape, NEG_BIG, F32)
        l_sc[...] = jnp.zeros(l_sc.shape, F32)
        acc_sc[...] = jnp.zeros(acc_sc.shape, F32)
        for sb in range(nsl):
            ls = slice(sb * LANES, (sb + 1) * LANES)
            q = q_ref[:, ls]
            qc_sc[2 * sb] = jnp.where(lo, q, jnp.zeros_like(q)) * scale
            qc_sc[2 * sb + 1] = jnp.where(lo, jnp.zeros_like(q), q) * scale
            if mode == "moba":
                qh, ql = _split(qf_ref[:, ls])
                km = km_ref[:, ls]
                own = qpos // MB_BLOCK
                for c in range(2):
                    kh, kl = _split(jnp.where(lo if c == 0 else ~lo, km, 0.0))
                    gate = _dot3_nt(kh, kl, qh, ql)
                    sel_sc[2 * sb + c] = _moba_select(gate, own, nblk)

    causal = kpos <= qpos
    for sb in range(nsl):
        ls = slice(sb * LANES, (sb + 1) * LANES)
        k = k_ref[:, ls]
        vt = vt_ref[ls, :]
        for c in range(2):
            ix = 2 * sb + c
            s = _dot_nt(k, qc_sc[ix])
            if mode == "moba":
                s = s + sel_sc[ix, pl.ds(ki * (tk // MB_BLOCK), 1), :]
            s = jnp.where(causal, s, NEG_BIG)
            m_prev = m_sc[ix]
            m_new = jnp.maximum(m_prev, jnp.max(s, axis=0, keepdims=True))
            alpha = jnp.exp(m_prev - m_new)
            p = jnp.exp(s - m_new)
            l_sc[ix] = alpha * l_sc[ix] + jnp.sum(p, axis=0, keepdims=True)
            vc = vt if mode == "diff" else vt[c * vrows:(c + 1) * vrows, :]
            acc_sc[ix] = alpha * acc_sc[ix] + _dot(vc, p.astype(BF16))
            m_sc[ix] = m_new

    last_ki = ((qi + 1) * tq - 1) // tk

    @pl.when(ki == last_ki)
    def _finish():
        for sb in range(nsl):
            o0 = acc_sc[2 * sb] / l_sc[2 * sb]
            o1 = acc_sc[2 * sb + 1] / l_sc[2 * sb + 1]
            if mode == "diff":
                lp = lam_ref[...]
                lam = (jnp.exp(jnp.sum(lp[0:1] * lp[1:2], axis=1, keepdims=True))
                       - jnp.exp(jnp.sum(lp[2:3] * lp[3:4], axis=1, keepdims=True)) + lam_init)
                o = o0 - lam * o1
                o = o * lax.rsqrt(jnp.mean(o * o, axis=0, keepdims=True) + LN_EPS) * g_ref[...] * (1.0 - lam_init)
            else:
                o = jnp.concatenate([o0, o1], axis=0)
            o_ref[:, sb * LANES:(sb + 1) * LANES] = o.T.astype(o_ref.dtype)


def _tri_steps(t, tq, tk):
    qs, ks = [], []
    for qi in range(t // tq):
        for ki in range(((qi + 1) * tq - 1) // tk + 1):
            qs.append(qi)
            ks.append(ki)
    return np.asarray(qs, np.int32), np.asarray(ks, np.int32)


def _attn_prompt(mode, q, k, vt, extra, batch, t, tq, tk, nsl, lam_init=0.0):
    n, w = q.shape
    nslab = w // LANES
    nblk = t // MB_BLOCK
    qs, ks = _tri_steps(t, tq, tk)
    nq, nk = t // tq, t // tk
    sw = nsl * LANES
    qspec = pl.BlockSpec((tq, sw), lambda b, h, s, qt, kt: (b * nq + qt[s], h))
    kspec = pl.BlockSpec((tk, sw), lambda b, h, s, qt, kt: (b * nk + kt[s], h))
    vspec = pl.BlockSpec((sw, tk), lambda b, h, s, qt, kt: (h, b * nk + kt[s]))
    if mode == "diff":
        extra_specs = [pl.BlockSpec((4, DA_HEAD_DIM), lambda b, h, s, qt, kt: (0, 0)),
                       pl.BlockSpec((LANES, 1), lambda b, h, s, qt, kt: (0, 0))]
        scratch = []
        vrows = LANES
    else:
        assert tk == MB_BLOCK
        extra_specs = [qspec, pl.BlockSpec((nblk, sw), lambda b, h, s, qt, kt: (b, h))]
        scratch = [pltpu.VMEM((2 * nsl, nblk, tq), F32)]
        vrows = HEAD_CHUNK
    kern = functools.partial(_attn_kernel, mode=mode, tq=tq, tk=tk, nblk=nblk, lam_init=lam_init)
    return pl.pallas_call(
        kern,
        grid_spec=pltpu.PrefetchScalarGridSpec(
            num_scalar_prefetch=2,
            grid=(batch, nslab // nsl, len(qs)),
            in_specs=[qspec, kspec, vspec] + extra_specs,
            out_specs=qspec,
            scratch_shapes=[pltpu.VMEM((2 * nsl, tq, LANES), BF16),
                            pltpu.VMEM((2 * nsl, 1, tq), F32), pltpu.VMEM((2 * nsl, 1, tq), F32),
                            pltpu.VMEM((2 * nsl, vrows, tq), F32)] + scratch,
        ),
        out_shape=jax.ShapeDtypeStruct((n, w), BF16),
        compiler_params=_cparams(("arbitrary", "arbitrary", "arbitrary")),
        name="attn_" + mode,
    )(jnp.asarray(qs), jnp.asarray(ks), q, k, vt, *extra)


def _stack_heads(q8, n_groups):
    t, w = q8.shape
    rows = n_groups * t
    qt = jnp.concatenate([q8] * n_groups, axis=0)
    rowgrp = lax.broadcasted_iota(jnp.int32, (rows, 1), 0) // t
    colgrp = lax.broadcasted_iota(jnp.int32, (1, w), 1) // (w // n_groups)
    diag = rowgrp == colgrp
    return jnp.where(diag, qt, 0.0), diag


def _softmax_update(s_list, v_list, m_prev, l_prev, acc_prev):
    m_cur = functools.reduce(jnp.maximum, [jnp.max(s, axis=1, keepdims=True) for s in s_list])
    m_new = jnp.maximum(m_prev, m_cur)
    alpha = jnp.exp(m_prev - m_new)
    l = alpha * l_prev
    acc = alpha * acc_prev
    for s, v in zip(s_list, v_list):
        p = jnp.exp(s - m_new)
        l = l + jnp.sum(p, axis=1, keepdims=True)
        acc = acc + _dot(p.astype(v.dtype), v)
    return m_new, l, acc


def _diff_dec_kernel(pt_ref, q_ref, kn_ref, vn_ref, lam_ref, g_ref, *refs, pp, tdec, lam_init):
    k_refs = refs[:pp]
    v_refs = refs[pp:2 * pp]
    o_ref = refs[2 * pp]
    nh = DA_HEADS
    rows = 2 * nh * tdec
    lane = lax.broadcasted_iota(jnp.int32, (1, LANES), 1)
    lo = lane < HEAD_CHUNK
    rowhead = lax.broadcasted_iota(jnp.int32, (rows, 1), 0) // (2 * tdec)

    q = q_ref[...] * (DA_HEAD_DIM ** -0.5)
    parts = []
    for h in range(nh):
        qh = q[:, h * LANES:(h + 1) * LANES]
        parts.append(jnp.where(lo, qh, 0.0))
        parts.append(jnp.where(lo, 0.0, qh))
    qs = jnp.concatenate(parts, axis=0)
    qb = qs.astype(BF16)
    page_rows = k_refs[0].shape[0]
    same = rowhead == lax.broadcasted_iota(jnp.int32, (1, page_rows), 1) % nh
    s_list = [jnp.where(same, _dot_nt(qb, k_refs[r][...].astype(BF16)), NEG_BIG) for r in range(pp)]
    v_list = [v_refs[r][...].astype(BF16) for r in range(pp)]

    kn = kn_ref[...]
    vn = vn_ref[...]
    knr = jnp.concatenate([kn[:, h * LANES:(h + 1) * LANES] for h in range(nh)], axis=0)
    vnr = jnp.concatenate([vn[:, h * LANES:(h + 1) * LANES] for h in range(nh)], axis=0)
    col = lax.broadcasted_iota(jnp.int32, (1, nh * tdec), 1)
    trow = lax.broadcasted_iota(jnp.int32, (rows, 1), 0) % tdec
    valid = (rowhead == col // tdec) & (col % tdec <= trow)
    s_list.append(jnp.where(valid, _dot_nt(qs, knr), NEG_BIG))
    v_list.append(vnr)

    _, l, acc = _softmax_update(s_list, v_list, jnp.full((rows, 1), NEG_BIG, F32),
                                jnp.zeros((rows, 1), F32), jnp.zeros((rows, LANES), F32))
    o = acc / l
    lp = lam_ref[...]
    lam = (jnp.exp(jnp.sum(lp[0:1] * lp[1:2], axis=1, keepdims=True))
           - jnp.exp(jnp.sum(lp[2:3] * lp[3:4], axis=1, keepdims=True)) + lam_init)
    outs = []
    for h in range(nh):
        oh = o[2 * h * tdec:(2 * h + 1) * tdec] - lam * o[(2 * h + 1) * tdec:(2 * h + 2) * tdec]
        outs.append(oh * lax.rsqrt(jnp.mean(oh * oh, axis=1, keepdims=True) + LN_EPS))
    o_ref[...] = jnp.concatenate(outs, axis=1) * g_ref[...] * (1.0 - lam_init)


def _page_spec(layer, idx, rows, w):
    return pl.BlockSpec((None, None, rows, w), lambda b, pt: (pt[b, idx], layer, 0, 0))


def _diff_dec(q, kn, vn, lam_p, g4, cache_k, cache_v, page_table, layer, lam_init):
    ns, w = q.shape
    bs, n_pages = page_table.shape
    tdec = ns // bs
    page_rows = cache_k.shape[2]
    row_spec = pl.BlockSpec((tdec, w), lambda b, pt: (b, 0))
    kern = functools.partial(_diff_dec_kernel, pp=n_pages, tdec=tdec, lam_init=lam_init)
    return pl.pallas_call(
        kern,
        grid_spec=pltpu.PrefetchScalarGridSpec(
            num_scalar_prefetch=1,
            grid=(bs,),
            in_specs=[row_spec, row_spec, row_spec,
                      pl.BlockSpec((4, DA_HEAD_DIM), lambda b, pt: (0, 0)),
                      pl.BlockSpec((1, w), lambda b, pt: (0, 0))]
            + [_page_spec(layer, r, page_rows, LANES) for r in range(n_pages)] * 2,
            out_specs=row_spec,
        ),
        out_shape=jax.ShapeDtypeStruct((ns, w), F32),
        compiler_params=_cparams(("arbitrary",)),
        name="diff_dec",
    )(page_table, q, kn, vn, lam_p, g4, *([cache_k] * n_pages), *([cache_v] * n_pages))


def _moba_dec_kernel(pt_ref, q_ref, kn_ref, vn_ref, *refs, ppb, tdec, nb):
    k_refs = refs[:nb * ppb]
    v_refs = refs[nb * ppb:2 * nb * ppb]
    o_ref = refs[2 * nb * ppb]
    scale = MB_HEAD_DIM ** -0.5
    rows = MB_HEADS * tdec

    qbd, _ = _stack_heads(q_ref[...], MB_HEADS)
    qb = (qbd * scale).astype(BF16)
    gates, ms, ls, accs = [], [], [], []
    for n in range(nb):
        s = jnp.concatenate([_dot(qb, k_refs[n * ppb + r][...].astype(BF16)) for r in range(ppb)], axis=1)
        page = s.shape[1] // ppb
        m_n = jnp.max(s, axis=1, keepdims=True)
        p = jnp.exp(s - m_n)
        acc = _dot_nt(p[:, :page].astype(BF16), v_refs[n * ppb][...].astype(BF16))
        for r in range(1, ppb):
            acc = acc + _dot_nt(p[:, r * page:(r + 1) * page].astype(BF16), v_refs[n * ppb + r][...].astype(BF16))
        gates.append(jnp.sum(s, axis=1, keepdims=True))
        ms.append(m_n)
        ls.append(jnp.sum(p, axis=1, keepdims=True))
        accs.append(acc)

    s_o = _dot_nt(qbd * scale, kn_ref[...])
    trow = lax.broadcasted_iota(jnp.int32, (rows, 1), 0) % tdec
    jcol = lax.broadcasted_iota(jnp.int32, (1, tdec), 1)
    s_o = jnp.where(jcol <= trow, s_o, NEG_BIG)
    m_o = jnp.max(s_o, axis=1, keepdims=True)
    p_o = jnp.exp(s_o - m_o)
    l_o = jnp.sum(p_o, axis=1, keepdims=True)
    a_o = _dot(p_o, vn_ref[...])
    sels = []
    for n in range(nb):
        rank = jnp.zeros((rows, 1), jnp.int32)
        for n2 in range(nb):
            if n2 == n:
                continue
            beats = (gates[n2] > gates[n]) | ((gates[n2] == gates[n]) & (n2 < n))
            rank = rank + jnp.where(beats, 1, 0)
        sels.append(rank < MB_TOPK)
    m = m_o
    for n in range(nb):
        m = jnp.maximum(m, jnp.where(sels[n], ms[n], NEG_BIG))
    w_o = jnp.exp(m_o - m)
    l = l_o * w_o
    acc = a_o * w_o
    for n in range(nb):
        w_n = jnp.where(sels[n], jnp.exp(ms[n] - m), 0.0)
        l = l + w_n * ls[n]
        acc = acc + w_n * accs[n]
    o = acc / l
    rowhead = lax.broadcasted_iota(jnp.int32, (rows, 1), 0) // tdec
    colhead = lax.broadcasted_iota(jnp.int32, (1, o.shape[1]), 1) // MB_HEAD_DIM
    o = jnp.where(rowhead == colhead, o, 0.0)
    o_ref[...] = jnp.sum(o.reshape(MB_HEADS, tdec, o.shape[1]), axis=0)


def _moba_dec(q, kn, vn, cache_k, cache_v, page_table, layer):
    ns, w = q.shape
    bs, n_pages = page_table.shape
    tdec = ns // bs
    page = cache_k.shape[3]
    ppb = MB_BLOCK // page
    nb = n_pages // ppb
    row_spec = pl.BlockSpec((tdec, w), lambda b, pt: (b, 0))
    kern = functools.partial(_moba_dec_kernel, ppb=ppb, tdec=tdec, nb=nb)
    return pl.pallas_call(
        kern,
        grid_spec=pltpu.PrefetchScalarGridSpec(
            num_scalar_prefetch=1,
            grid=(bs,),
            in_specs=[row_spec, row_spec, row_spec]
            + [_page_spec(layer, r, w, page) for r in range(n_pages)] * 2,
            out_specs=row_spec,
        ),
        out_shape=jax.ShapeDtypeStruct((ns, w), F32),
        compiler_params=_cparams(("arbitrary",)),
        name="moba_dec",
    )(page_table, q, kn, vn, *([cache_k] * n_pages), *([cache_v] * n_pages))


def _mix_kernel(oa_ref, ob_ref, sg_ref, x_ref, gt_ref, sh2_ref, sc2_ref, g_ref, b_ref,
                wa_ref, wb_ref, wo_ref, x1_ref, h2_ref, h2b_ref, *, alpha):
    ya = _dot(oa_ref[...].astype(BF16), wa_ref[...])
    yb = _dot(ob_ref[...].astype(BF16), wb_ref[...])
    merged = sg_ref[:, :D_MODEL] * ya + sg_ref[:, D_MODEL:] * yb
    y = _dot(merged.astype(BF16), wo_ref[...])
    x1 = _layer_norm(alpha * x_ref[...] + gt_ref[...] * y, g_ref[...], b_ref[...])
    x1_ref[...] = x1
    h2 = x1 * (1.0 + sc2_ref[...]) + sh2_ref[...]
    h2_ref[...] = h2
    h2b_ref[...] = h2.astype(BF16)


def _mix_out(oa, ob, sg, x, mod1, mod2, per_token, tokens_per_seq, ln_g, ln_b, wa, wb, wo, alpha, tm):
    n, d = x.shape
    tiles_per_seq = max(tokens_per_seq // tm, 1)
    row = pl.BlockSpec((tm, d), lambda i: (i, 0))
    half = pl.BlockSpec((tm, QKV_W), lambda i: (i, 0))
    vec = pl.BlockSpec((1, d), lambda i: (0, 0))
    return pl.pallas_call(
        functools.partial(_mix_kernel, alpha=alpha),
        grid=(n // tm,),
        in_specs=[half, half, pl.BlockSpec((tm, 2 * d), lambda i: (i, 0)), row,
                  _mod_spec(per_token, tm, tiles_per_seq, 2),
                  _mod_spec(per_token, tm, tiles_per_seq, 0),
                  _mod_spec(per_token, tm, tiles_per_seq, 1),
                  vec, vec,
                  pl.BlockSpec((QKV_W, d), lambda i: (0, 0)),
                  pl.BlockSpec((QKV_W, d), lambda i: (0, 0)),
                  pl.BlockSpec((d, d), lambda i: (0, 0))],
        out_specs=[row, row, row],
        out_shape=[jax.ShapeDtypeStruct((n, d), F32), jax.ShapeDtypeStruct((n, d), F32),
                   jax.ShapeDtypeStruct((n, d), BF16)],
        compiler_params=_cparams(("arbitrary",)),
        name="mix_out",
    )(oa, ob, sg, x, mod1, mod2, mod2, ln_g.reshape(1, d), ln_b.reshape(1, d), wa, wb, wo)


PEER_NEXT = PEER_TOPK + 1
PEER_GS = 4
EXP_CLAMP = 80.0


def _top_values(x, count):
    vals = []
    for _ in range(count):
        m = jnp.max(x, axis=0)
        vals.append(m)
        x = jnp.where(x == m[None], -jnp.inf, x)
    return vals


def _route_kernel(h2_ref, wqh_ref, wql_ref, kh_ref, kl_ref, thr_ref, e1_ref, e2_ref, s1_sc, s2_sc):
    hh, hl = _split(h2_ref[...])
    q = _dot3(hh, hl, wqh_ref[...], wql_ref[...])
    qh, ql = _split(q)
    tm = q.shape[0]
    nk, nh = PEER_NKEYS, PEER_HEADS
    hw = q.shape[1] // nh
    s2p = []
    for h in range(nh):
        cs = slice(h * hw, (h + 1) * hw)
        sh = _dot3_nt(kh_ref[h], kl_ref[h], qh[:, cs], ql[:, cs])
        s1_sc[:, h, :] = sh[:nk]
        s2_sc[:, h, :] = sh[nk:]
        s2p.append(sh[nk:])
    s1 = s1_sc[...]
    s2 = s2_sc[...]
    s2p = jnp.stack(s2p, axis=0)
    a = _top_values(s1, PEER_NEXT)
    b = _top_values(s2, PEER_NEXT)
    cands = [a[i] + b[j] for i in range(PEER_NEXT) for j in range(PEER_NEXT) if (i + 1) * (j + 1) <= PEER_NEXT]
    x = jnp.stack(cands, axis=0)
    t = _top_values(x, PEER_NEXT)
    tau = 0.5 * (t[PEER_TOPK - 1] + t[PEER_TOPK])
    m0 = a[0] + b[0]
    z = jnp.zeros_like(m0)
    for cnd in cands:
        z = z + jnp.where(cnd > tau, jnp.exp(cnd - m0), 0.0)
    thr_ref[...] = jnp.exp(jnp.minimum(tau[None] - s1 - b[0][None], EXP_CLAMP))
    e1_ref[...] = jnp.exp(s1 - a[0][None]) / z[None]
    e2_ref[...] = jnp.exp(s2p - jnp.max(s2p, axis=1, keepdims=True))


def _peer_route(h2, wqh, wql, kmats, tm):
    n, d = h2.shape
    nk, nh = PEER_NKEYS, PEER_HEADS
    wspec = pl.BlockSpec((d, d), lambda i: (0, 0))
    kspec = pl.BlockSpec((nh, 2 * nk, d // nh), lambda i: (0, 0, 0))
    return pl.pallas_call(
        _route_kernel,
        grid=(n // tm,),
        in_specs=[pl.BlockSpec((tm, d), lambda i: (i, 0)), wspec, wspec, kspec, kspec],
        scratch_shapes=[pltpu.VMEM((nk, nh, tm), F32), pltpu.VMEM((nk, nh, tm), F32)],
        out_specs=[pl.BlockSpec((nk, nh, tm), lambda i: (0, 0, i)),
                   pl.BlockSpec((nk, nh, tm), lambda i: (0, 0, i)),
                   pl.BlockSpec((nh, nk, tm), lambda i: (0, 0, i))],
        out_shape=[jax.ShapeDtypeStruct((nk, nh, n), F32), jax.ShapeDtypeStruct((nk, nh, n), F32),
                   jax.ShapeDtypeStruct((nh, nk, n), F32)],
        compiler_params=_cparams(("arbitrary",)),
        name="peer_route",
    )(h2, wqh, wql, *kmats)


def _peer_kmats(keys):
    nh, _, nk, hd = keys.shape
    zero = jnp.zeros((nh, nk, hd), F32)
    m = jnp.concatenate([jnp.concatenate([keys[:, 0], zero], axis=2),
                         jnp.concatenate([zero, keys[:, 1]], axis=2)], axis=1)
    return _split(m)


def _gelu(x):
    return 0.5 * x * (1.0 + lax.erf(x * (2.0 ** -0.5)))


def _peer_kernel(h2b_ref, u_ref, vtp_ref, vta_ref, vtl_ref, thr_ref, e1_ref, e2_ref, x1_ref, gt_ref, g_ref, b_ref,
                 o_ref, acc_sc, pa_sc, pb_sc, *, gs, alpha):
    j = pl.program_id(1)
    nk, nh = PEER_NKEYS, PEER_HEADS
    tm = h2b_ref.shape[0]

    @pl.when(j == 0)
    def _init():
        acc_sc[...] = jnp.zeros(acc_sc.shape, F32)
        pb_sc[...] = jnp.zeros(pb_sc.shape, BF16)

    h2b = h2b_ref[...]
    tcw = min(LANES, tm)

    npiece = max(tm // (2 * LANES), 1)
    pw = tm // npiece

    def build(half, p_sc, vt_prev_ref, p_prev_sc):
        for gg in range(gs):
            if gg * npiece % gs == 0:
                cs = slice(gg * npiece // gs * pw, (gg * npiece // gs + 1) * pw)
                acc_sc[:, cs] += _dot(vt_prev_ref[...], p_prev_sc[:, cs])
            g = (2 * j + half) * gs + gg
            r0 = (half * gs + gg) * nk
            at = _dot_nt(u_ref[r0:r0 + nk, :], h2b)
            thr8 = thr_ref[g]
            e18 = e1_ref[g]
            for tc in range(tm // tcw):
                sl = slice(tc * tcw, (tc + 1) * tcw)
                w = jnp.zeros((nk, tcw), F32)
                for h in range(nh):
                    e2 = e2_ref[h, :, sl]
                    w = w + jnp.where(e2 > thr8[h:h + 1, sl], e2, 0.0) * e18[h:h + 1, sl]
                p_sc[gg * nk:(gg + 1) * nk, sl] = (w * _gelu(at[:, sl])).astype(BF16)

    build(0, pa_sc, vtp_ref, pb_sc)
    build(1, pb_sc, vta_ref, pa_sc)

    @pl.when(j == pl.num_programs(1) - 1)
    def _finish():
        f = (acc_sc[...] + _dot(vtl_ref[...], pb_sc[...])).T
        o_ref[...] = _layer_norm(alpha * x1_ref[...] + gt_ref[...] * f, g_ref[...], b_ref[...])


def _peer_dense(h2b, u_bf, vt_blk, thr, e1, e2, x1, mod2, per_token, tokens_per_seq, ln_g, ln_b, alpha, tm, gs):
    n, d = x1.shape
    nk, nh = PEER_NKEYS, PEER_HEADS
    nhalf = vt_blk.shape[0]
    hw = gs * nk
    tiles_per_seq = max(tokens_per_seq // tm, 1)
    if per_token:
        gt_spec = pl.BlockSpec((tm, d), lambda i, j: (i, 2))
    else:
        gt_spec = pl.BlockSpec((None, 1, d), lambda i, j: (i // tiles_per_seq, 0, 2))
    return pl.pallas_call(
        functools.partial(_peer_kernel, gs=gs, alpha=alpha),
        grid=(n // tm, nhalf // 2),
        in_specs=[pl.BlockSpec((tm, d), lambda i, j: (i, 0)),
                  pl.BlockSpec((2 * hw, d), lambda i, j: (j, 0)),
                  pl.BlockSpec((None, d, hw), lambda i, j: (jnp.maximum(2 * j - 1, 0), 0, 0)),
                  pl.BlockSpec((None, d, hw), lambda i, j: (2 * j, 0, 0)),
                  pl.BlockSpec((None, d, hw), lambda i, j: (nhalf - 1, 0, 0)),
                  pl.BlockSpec((nk, nh, tm), lambda i, j: (0, 0, i)),
                  pl.BlockSpec((nk, nh, tm), lambda i, j: (0, 0, i)),
                  pl.BlockSpec((nh, nk, tm), lambda i, j: (0, 0, i)),
                  pl.BlockSpec((tm, d), lambda i, j: (i, 0)),
                  gt_spec,
                  pl.BlockSpec((1, d), lambda i, j: (0, 0)),
                  pl.BlockSpec((1, d), lambda i, j: (0, 0))],
        out_specs=pl.BlockSpec((tm, d), lambda i, j: (i, 0)),
        out_shape=jax.ShapeDtypeStruct((n, d), F32),
        scratch_shapes=[pltpu.VMEM((d, tm), F32), pltpu.VMEM((hw, tm), BF16), pltpu.VMEM((hw, tm), BF16)],
        compiler_params=_cparams(("arbitrary", "arbitrary")),
        name="peer_dense",
    )(h2b, u_bf, vt_blk, vt_blk, vt_blk, thr, e1, e2, x1, mod2, ln_g.reshape(1, d), ln_b.reshape(1, d))


def _pick_tile(n, pref):
    t = pref
    while n % t:
        t //= 2
    return t


def kernel(x_prompt, x_sample, cache_dk, cache_dv, cache_mk, cache_mv, page_table, c_prompt, c_sample,
           ln_in_g, ln_in_b, w_ada_mix, b_ada_mix, w_in, diff_lambda, diff_subln_g, w_branch_a, w_branch_b,
           w_out, ln1_g, ln1_b, w_ada_ffn, b_ada_ffn, peer_wq, peer_keys, peer_u, peer_v, ln2_g, ln2_b):
    batch, seq, d = x_prompt.shape
    dec_batch, dec_seq, _ = x_sample.shape
    depth = w_in.shape[0]
    n_pool, _, page, _, _ = cache_dk.shape
    past_len = page_table.shape[1] * page
    alpha = (2 * depth) ** 0.25
    assert seq % MB_BLOCK == 0 and past_len % MB_BLOCK == 0 and MB_BLOCK % page == 0
    assert dec_seq <= MB_BLOCK and dec_seq % SUBLANES == 0

    w_in_bf = w_in.astype(BF16)
    wa_bf = w_branch_a.astype(BF16)
    wb_bf = w_branch_b.astype(BF16)
    wo_bf = w_out.astype(BF16)
    u_bf = peer_u.astype(BF16)
    n_exp = peer_v.shape[1]
    half_w = PEER_GS * PEER_NKEYS
    vt_blk = jnp.swapaxes(peer_v.reshape(depth, n_exp // half_w, half_w, d), 2, 3).astype(BF16)
    wq_split = [_split(peer_wq[l]) for l in range(depth)]
    kmats = [_peer_kmats(peer_keys[l]) for l in range(depth)]
    wvt_bf = jnp.stack([jnp.swapaxes(w_in[:, :, 2 * QKV_W:3 * QKV_W], 1, 2),
                        jnp.swapaxes(w_in[:, :, 5 * QKV_W:6 * QKV_W], 1, 2)], axis=1).astype(BF16)
    cdk = cache_dk.reshape(n_pool, depth, page * DA_HEADS, 2 * DA_HEAD_DIM)
    cdv = cache_dv.reshape(n_pool, depth, page * DA_HEADS, 2 * DA_HEAD_DIM)
    cmk = jnp.transpose(cache_mk, (0, 1, 3, 4, 2)).reshape(n_pool, depth, QKV_W, page)
    cmv = jnp.transpose(cache_mv, (0, 1, 3, 4, 2)).reshape(n_pool, depth, QKV_W, page)
    g4 = jnp.tile(diff_subln_g, (1, DA_HEADS))

    r = batch + dec_batch
    r_pad = -(-r // SUBLANES) * SUBLANES
    c_all = jnp.concatenate([c_prompt, c_sample, jnp.zeros((r_pad - r, d), F32)], axis=0)
    mods_mix = _ada(c_all, w_ada_mix, b_ada_mix)
    mods_ffn = _ada(c_all, w_ada_ffn, b_ada_ffn)

    def run_group(x, row0, nseq, tlen, pos, is_prompt):
        n = nseq * tlen
        per_token = not is_prompt
        tm = _pick_tile(n, 512)
        if is_prompt:
            tm = min(tm, _pick_tile(tlen, 512))
            tabs = _rope_tables(pos)
        else:
            tabs = _rope_tables(jnp.tile(pos, tm // tlen))
        xs = _ln_rows(x.reshape(n, d), ln_in_g, ln_in_b, tm)
        rows = {k: [] for k in ("dk", "dv", "mk", "mv")}
        for l in range(depth):
            lam_init = 0.8 - 0.6 * math.exp(-0.3 * l)
            mod1 = _mod_operand(mods_mix[l, row0:row0 + nseq], per_token, tlen)
            mod2 = _mod_operand(mods_ffn[l, row0:row0 + nseq], per_token, tlen)
            (qa, ka, va, qb, kb, vb, qab, kab, vat, qbb, kbb, vbt, sg) = _inproj(
                xs, mod1, per_token, tlen, w_in_bf[l], wvt_bf[l], tabs, tm)
            if is_prompt:
                tq = _pick_tile(tlen, 512)
                oa = _attn_prompt("diff", qab, kab, vat,
                                  (diff_lambda[l], diff_subln_g[l].reshape(LANES, 1)),
                                  nseq, tlen, tq, tq, 2, lam_init)
                km = _kmean(kb)
                ob = _attn_prompt("moba", qbb, kbb, vbt, (qb, km), nseq, tlen, _pick_tile(tlen, 1024), MB_BLOCK, 2)
            else:
                oa = _diff_dec(qa, ka, va, diff_lambda[l], g4[l].reshape(1, QKV_W), cdk, cdv, page_table, l,
                               lam_init)
                ob = _moba_dec(qb, kb, vb, cmk, cmv, page_table, l)
            x1, h2, h2b = _mix_out(oa, ob, sg, xs, mod1, mod2, per_token, tlen, ln1_g[l], ln1_b[l],
                                   wa_bf[l], wb_bf[l], wo_bf[l], alpha, tm)
            thr, e1, e2 = _peer_route(h2, wq_split[l][0], wq_split[l][1], kmats[l], _pick_tile(n, 256))
            xs = _peer_dense(h2b, u_bf[l], vt_blk[l], thr, e1, e2, x1, mod2, per_token, tlen,
                             ln2_g[l], ln2_b[l], alpha, tm, PEER_GS)
            rows["dk"].append(ka)
            rows["dv"].append(va)
            rows["mk"].append(kb)
            rows["mv"].append(vb)
        y = xs.reshape(nseq, tlen, d)
        dk = jnp.stack(rows["dk"], 0).reshape(depth, nseq, tlen, DA_HEADS, 2 * DA_HEAD_DIM).swapaxes(0, 1)
        dv = jnp.stack(rows["dv"], 0).reshape(depth, nseq, tlen, DA_HEADS, 2 * DA_HEAD_DIM).swapaxes(0, 1)
        mk = jnp.stack(rows["mk"], 0).reshape(depth, nseq, tlen, MB_HEADS, MB_HEAD_DIM).swapaxes(0, 1)
        mv = jnp.stack(rows["mv"], 0).reshape(depth, nseq, tlen, MB_HEADS, MB_HEAD_DIM).swapaxes(0, 1)
        return y, dk, dv, mk, mv

    yp, dkp, dvp, mkp, mvp = run_group(x_prompt, 0, batch, seq, jnp.arange(seq), True)
    ys, dks, dvs, mks, mvs = run_group(x_sample, batch, dec_batch, dec_seq, past_len + jnp.arange(dec_seq), False)
    return (yp, ys, dkp, dvp, mkp, mvp, dks, dvs, mks, mvs)
```
